```python
import math
import jax, jax.numpy as jnp
from jax import lax
import numpy as np

D_MODEL = 1024
BATCH = 8
SEQ = 2048
DEPTH = 1

D_MIX = D_MODEL
CONV_WIDTH = D_MIX // 2
CONV_GROUPS = 8
CONV_KERNEL = 31
GDN_WIDTH = D_MIX - CONV_WIDTH
GDN_HEAD_DIM = 128
GDN_HEADS = GDN_WIDTH // GDN_HEAD_DIM
GDN_SHORT_CONV = 4
GDN_CHUNK = 64
N_IN = 2 * CONV_WIDTH + 4 * GDN_WIDTH + 2 * GDN_HEADS
D_FF = -(-8 * D_MODEL // (3 * 256)) * 256
N_MOD = 6
EPS = 1e-6

kernel_name = "hybrid_conformer_gdn_adaln_block"


def rms_norm(x, w):
    xf = x.astype(jnp.float32)
    y = xf * lax.rsqrt(jnp.mean(xf * xf, axis=-1, keepdims=True) + EPS)
    return (y * w.astype(jnp.float32)).astype(x.dtype)


def modulate(h, shift, scale):
    return h * (1 + scale[:, None, :]) + shift[:, None, :]


def causal_depthwise_conv(x, w):
    k_len, ch = w.shape
    return lax.conv_general_dilated(
        x, w[:, None, :].astype(x.dtype), window_strides=(1,), padding=((k_len - 1, 0),),
        dimension_numbers=("NWC", "WIO", "NWC"), feature_group_count=ch)


def l2norm(t):
    return t * lax.rsqrt(jnp.sum(t * t, axis=-1, keepdims=True) + EPS)


def conformer_conv_group(a, gate, w_dw, b_dw, gn_w, gn_b):
    u = a * jax.nn.sigmoid(gate)
    u = causal_depthwise_conv(u, w_dw) + b_dw
    bsz, seq, ch = u.shape
    uf = u.astype(jnp.float32).reshape(bsz, seq, CONV_GROUPS, ch // CONV_GROUPS)
    mu = jnp.mean(uf, axis=-1, keepdims=True)
    var = jnp.mean(jnp.square(uf - mu), axis=-1, keepdims=True)
    un = ((uf - mu) * lax.rsqrt(var + EPS)).reshape(bsz, seq, ch)
    un = un * gn_w.astype(jnp.float32) + gn_b.astype(jnp.float32)
    return jax.nn.silu(un).astype(a.dtype)


def chunk_gated_delta_rule(q, k, v, g, beta):
    bsz, seq, nh, dk = q.shape
    dv = v.shape[-1]
    cl = GDN_CHUNK
    nc = seq // cl
    q = q * (dk ** -0.5)

    def to_chunks(t):
        return t.reshape(bsz, nc, cl, nh, -1).transpose(1, 0, 3, 2, 4)

    qc, kc, vc = to_chunks(q), to_chunks(k), to_chunks(v)
    gc = g.reshape(bsz, nc, cl, nh).transpose(1, 0, 3, 2)
    bc = beta.reshape(bsz, nc, cl, nh).transpose(1, 0, 3, 2)
    gcum = jnp.cumsum(gc, axis=-1)
    causal = jnp.tril(jnp.ones((cl, cl), dtype=bool))
    strict = jnp.tril(jnp.ones((cl, cl), dtype=bool), -1)
    diff = gcum[..., :, None] - gcum[..., None, :]
    decay = jnp.where(causal, jnp.exp(jnp.where(causal, diff, 0.0)), 0.0)

    k_beta = kc * bc[..., None]
    a_mat = jnp.where(strict, jnp.einsum("nbhcd,nbhed->nbhce", k_beta, kc) * decay, 0.0)
    eye = jnp.eye(cl, dtype=q.dtype)
    t_mat = lax.linalg.triangular_solve(eye + a_mat, jnp.broadcast_to(eye, a_mat.shape),
                                        left_side=True, lower=True, unit_diagonal=True)
    u = jnp.einsum("nbhce,nbhev->nbhcv", t_mat, vc * bc[..., None])
    w = jnp.einsum("nbhce,nbhek->nbhck", t_mat, k_beta * jnp.exp(gcum)[..., None])
    qk = jnp.where(causal, jnp.einsum("nbhcd,nbhed->nbhce", qc, kc) * decay, 0.0)
    q_dec = qc * jnp.exp(gcum)[..., None]
    k_dec = kc * jnp.exp(gcum[..., -1:] - gcum)[..., None]
    chunk_decay = jnp.exp(gcum[..., -1])

    def step(state, inp):
        q_i, k_i, w_i, u_i, qk_i, cd_i = inp
        v_new = u_i - jnp.einsum("bhck,bhkv->bhcv", w_i, state)
        o_i = jnp.einsum("bhck,bhkv->bhcv", q_i, state) + jnp.einsum("bhce,bhev->bhcv", qk_i, v_new)
        state = state * cd_i[..., None, None] + jnp.einsum("bhck,bhcv->bhkv", k_i, v_new)
        return state, o_i

    state0 = jnp.zeros((bsz, nh, dk, dv), dtype=q.dtype)
    _, o = lax.scan(step, state0, (q_dec, k_dec, w, u, qk, chunk_decay))
    return o.transpose(1, 0, 3, 2, 4).reshape(bsz, seq, nh, dv)


def gated_deltanet_group(q, k, v, z, beta_logit, alpha_logit, w_sc, a_log, dt_bias, norm_w):
    bsz, seq, _ = q.shape
    out_dtype = q.dtype
    qkv = jax.nn.silu(causal_depthwise_conv(jnp.concatenate([q, k, v], axis=-1), w_sc))
    q, k, v = jnp.split(qkv.astype(jnp.float32), 3, axis=-1)
    shp = (bsz, seq, GDN_HEADS, GDN_HEAD_DIM)
    q = l2norm(q.reshape(shp))
    k = l2norm(k.reshape(shp))
    v = v.reshape(shp)
    beta = jax.nn.sigmoid(beta_logit.astype(jnp.float32))
    g = -jnp.exp(a_log.astype(jnp.float32)) * jax.nn.softplus(
        alpha_logit.astype(jnp.float32) + dt_bias.astype(jnp.float32))
    o = chunk_gated_delta_rule(q, k, v, g, beta)
    o = o * lax.rsqrt(jnp.mean(o * o, axis=-1, keepdims=True) + EPS) * norm_w.astype(jnp.float32)
    o = o * jax.nn.silu(z.astype(jnp.float32).reshape(shp))
    return o.reshape(bsz, seq, GDN_WIDTH).astype(out_dtype)


def setup_inputs(seed: int = 0) -> dict:
    key = jax.random.key(seed)
    ks = jax.random.split(key, 24)
    f32 = jnp.float32
    nrm = lambda k, shp, s: jax.random.normal(k, shp, f32) * s
    dt = jnp.exp(jax.random.uniform(ks[12], (DEPTH, GDN_HEADS), f32,
                                    minval=math.log(1e-3), maxval=math.log(0.1)))
    return {
        "x": nrm(ks[0], (BATCH, SEQ, D_MODEL), 1.0),
        "c": nrm(ks[1], (BATCH, D_MODEL), 1.0),
        "w_ada": nrm(ks[2], (DEPTH, D_MODEL, N_MOD * D_MODEL), 0.5 * D_MODEL ** -0.5),
        "b_ada": nrm(ks[3], (DEPTH, N_MOD * D_MODEL), 0.02),
        "norm_mix_w": 1.0 + nrm(ks[4], (DEPTH, D_MODEL), 0.02),
        "w_in": nrm(ks[5], (DEPTH, D_MODEL, N_IN), D_MODEL ** -0.5),
        "conv_w": nrm(ks[6], (DEPTH, CONV_KERNEL, CONV_WIDTH), CONV_KERNEL ** -0.5),
        "conv_b": nrm(ks[7], (DEPTH, CONV_WIDTH), 0.02),
        "conv_gn_w": 1.0 + nrm(ks[8], (DEPTH, CONV_WIDTH), 0.02),
        "conv_gn_b": nrm(ks[9], (DEPTH, CONV_WIDTH), 0.02),
        "gdn_conv_w": nrm(ks[10], (DEPTH, GDN_SHORT_CONV, 3 * GDN_WIDTH), GDN_SHORT_CONV ** -0.5),
        "gdn_a_log": jnp.log(jax.random.uniform(ks[11], (DEPTH, GDN_HEADS), f32, minval=1.0, maxval=16.0)),
        "gdn_dt_bias": dt + jnp.log(-jnp.expm1(-dt)),
        "gdn_norm_w": 1.0 + nrm(ks[13], (DEPTH, GDN_HEAD_DIM), 0.02),
        "w_out": nrm(ks[14], (DEPTH, D_MIX, D_MODEL), D_MIX ** -0.5),
        "norm_ffn_w": 1.0 + nrm(ks[15], (DEPTH, D_MODEL), 0.02),
        "w_ffn_in": nrm(ks[16], (DEPTH, D_MODEL, 2 * D_FF), D_MODEL ** -0.5),
        "w_ffn_out": nrm(ks[17], (DEPTH, D_FF, D_MODEL), D_FF ** -0.5),
        "norm_final_w": 1.0 + nrm(ks[18], (D_MODEL,), 0.02),
    }


def reference(x, c, w_ada, b_ada, norm_mix_w, w_in, conv_w, conv_b, conv_gn_w, conv_gn_b,
              gdn_conv_w, gdn_a_log, gdn_dt_bias, gdn_norm_w, w_out, norm_ffn_w,
              w_ffn_in, w_ffn_out, norm_final_w):
    split_at = list(np.cumsum([CONV_WIDTH, CONV_WIDTH, GDN_WIDTH, GDN_WIDTH, GDN_WIDTH, GDN_WIDTH, GDN_HEADS]))
    c_act = jax.nn.silu(c)
    for layer in range(DEPTH):
        mod = c_act @ w_ada[layer] + b_ada[layer]
        sh1, sc1, gt1, sh2, sc2, gt2 = jnp.split(mod, N_MOD, axis=-1)

        h = modulate(rms_norm(x, norm_mix_w[layer]), sh1, sc1)
        p = h @ w_in[layer]
        a_in, a_gate, q, k, v, z, beta_logit, alpha_logit = jnp.split(p, split_at, axis=-1)
        out_a = conformer_conv_group(a_in, a_gate, conv_w[layer], conv_b[layer],
                                     conv_gn_w[layer], conv_gn_b[layer])
        out_b = gated_deltanet_group(q, k, v, z, beta_logit, alpha_logit, gdn_conv_w[layer],
                                     gdn_a_log[layer], gdn_dt_bias[layer], gdn_norm_w[layer])
        mix = jnp.concatenate([out_a, out_b], axis=-1) @ w_out[layer]
        x = x + gt1[:, None, :] * mix

        h = modulate(rms_norm(x, norm_ffn_w[layer]), sh2, sc2)
        f_gate, f_up = jnp.split(h @ w_ffn_in[layer], 2, axis=-1)
        ffn = (jax.nn.silu(f_gate) * f_up) @ w_ffn_out[layer]
        x = x + gt2[:, None, :] * ffn
    return rms_norm(x, norm_final_w)
```

```python
import functools

import jax
import jax.numpy as jnp
import numpy as np
from jax import lax
from jax.experimental import pallas as pl
from jax.experimental.pallas import tpu as pltpu

F32 = jnp.float32
BF16 = jnp.bfloat16

D_MODEL = 1024
CONV_WIDTH = 512
CONV_GROUPS = 8
CONV_KERNEL = 31
GDN_WIDTH = 512
GDN_HEAD_DIM = 128
GDN_HEADS = 4
GDN_SHORT_CONV = 4
GDN_CHUNK = 64
D_FF = 2816
N_MOD = 6
EPS = 1e-6

LANES = 128
SUBLANES = 8
MXU_DIM = 256

CONV_HALO = 32
SC_HALO = SUBLANES
TM_IN = 256
GDN_ROWS = 256
TM_OUT = 512
VMEM_LIMIT = 56 * 1024 * 1024


def _silu(v):
    return v * jax.nn.sigmoid(v)


def _dot(a, b):
    return jnp.dot(a.astype(BF16), b.astype(BF16), preferred_element_type=F32)


def _adaln_kernel(c_ref, w_ref, b_ref, o_ref):
    c = c_ref[...]
    o_ref[...] = jnp.dot(_silu(c), w_ref[...], preferred_element_type=F32,
                         precision=lax.Precision.HIGHEST) + b_ref[...]


def _adaln(c, w, b):
    bsz, d = c.shape
    n = w.shape[1]
    tn = 1024
    return pl.pallas_call(
        _adaln_kernel,
        out_shape=jax.ShapeDtypeStruct((bsz, n), F32),
        grid=(n // tn,),
        in_specs=[pl.BlockSpec((bsz, d), lambda j: (0, 0)),
                  pl.BlockSpec((d, tn), lambda j: (0, j)),
                  pl.BlockSpec((1, tn), lambda j: (0, j))],
        out_specs=pl.BlockSpec((bsz, tn), lambda j: (0, j)),
        compiler_params=pltpu.CompilerParams(dimension_semantics=("arbitrary",),
                                             vmem_limit_bytes=VMEM_LIMIT),
        name="adaln",
    )(c, w, b.reshape(1, n))


def _mix_in_kernel(x_ref, mod_ref, nw_ref, w_ag_ref, w_qkv_ref, w_z_ref, w_ba_ref,
                   cw_ref, cb_ref, gnw_ref, gnb_ref, gmat_ref, scw_ref, ones_ref,
                   alog_ref, dtb_ref, tri_ref,
                   oa_ref, q_ref, k_ref, v_ref, z_ref, gc_ref, gr_ref,
                   ubuf, qbuf):
    i = pl.program_id(1)
    tm = x_ref.shape[1]

    @pl.when(i == 0)
    def _():
        ubuf[0:CONV_HALO, :] = jnp.zeros((CONV_HALO, CONV_WIDTH), F32)
        qbuf[0:SC_HALO, :] = jnp.zeros((SC_HALO, 3 * GDN_WIDTH), F32)

    x = x_ref[0]
    ms = jnp.mean(x * x, axis=-1, keepdims=True)
    h = x * lax.rsqrt(ms + EPS) * nw_ref[...]
    h = h * (1.0 + mod_ref[0, 1:2, :]) + mod_ref[0, 0:1, :]
    hb = h.astype(BF16)

    ag = jnp.dot(hb, w_ag_ref[...], preferred_element_type=F32)
    u = ag[:, :CONV_WIDTH] * jax.nn.sigmoid(ag[:, CONV_WIDTH:])
    ubuf[CONV_HALO:CONV_HALO + tm, :] = u
    base = CONV_HALO - (CONV_KERNEL - 1)
    for hp in range(CONV_WIDTH // MXU_DIM):
        half = slice(hp * MXU_DIM, (hp + 1) * MXU_DIM)
        ys = []
        for j in range(hp * MXU_DIM // LANES, (hp + 1) * MXU_DIM // LANES):
            ls = slice(j * LANES, (j + 1) * LANES)
            acc = cb_ref[:, ls] + cw_ref[0:1, ls] * ubuf[base:base + tm, ls]
            for t in range(1, CONV_KERNEL):
                acc = acc + cw_ref[t:t + 1, ls] * ubuf[base + t:base + t + tm, ls]
            ys.append(acc)
        y = jnp.concatenate(ys, axis=-1)
        mu = _dot(y, gmat_ref[...])
        d = y - mu
        var = _dot(d * d, gmat_ref[...])
        un = d * lax.rsqrt(var + EPS) * gnw_ref[:, half] + gnb_ref[:, half]
        oa_ref[0, :, half] = _silu(un).astype(oa_ref.dtype)
    ubuf[0:CONV_HALO, :] = ubuf[tm:tm + CONV_HALO, :]

    qkv = jnp.dot(hb, w_qkv_ref[...], preferred_element_type=F32)
    qbuf[SC_HALO:SC_HALO + tm, :] = qkv
    sbase = SC_HALO - (GDN_SHORT_CONV - 1)
    outs = (q_ref, k_ref, v_ref)
    for part in range(3):
        for hp in range(GDN_WIDTH // MXU_DIM):
            ls = slice(part * GDN_WIDTH + hp * MXU_DIM, part * GDN_WIDTH + (hp + 1) * MXU_DIM)
            acc = scw_ref[0:1, ls] * qbuf[sbase:sbase + tm, ls]
            for t in range(1, GDN_SHORT_CONV):
                acc = acc + scw_ref[t:t + 1, ls] * qbuf[sbase + t:sbase + t + tm, ls]
            a = _silu(acc)
            if part < 2:
                ss = _dot(a * a, ones_ref[...])
                a = a * lax.rsqrt(ss + EPS)
            outs[part][0, :, hp * MXU_DIM:(hp + 1) * MXU_DIM] = a
    qbuf[0:SC_HALO, :] = qbuf[tm:tm + SC_HALO, :]

    z_ref[0] = jnp.dot(hb, w_z_ref[...], preferred_element_type=F32)

    ba = jnp.dot(hb, w_ba_ref[...], preferred_element_type=F32)
    beta = jax.nn.sigmoid(ba[:, :LANES])
    al = ba[:, LANES:] + dtb_ref[...]
    sp = jnp.maximum(al, 0.0) + jnp.log1p(jnp.exp(-jnp.abs(al)))
    g = -jnp.exp(alog_ref[...]) * sp
    gcum = jnp.dot(tri_ref[...], g, preferred_element_type=F32, precision=lax.Precision.HIGHEST)
    lane = lax.broadcasted_iota(jnp.int32, (tm, LANES), 1)
    packed = jnp.where(lane < GDN_HEADS, gcum, pltpu.roll(beta, GDN_HEADS, 1))
    gc_ref[0] = packed
    gr_ref[0] = packed.T[0:SUBLANES, :]


def _mix_in(x, mod, nw, w_ag, w_qkv, w_z, w_ba, cw, cb, gnw, gnb, gmat, scw, ones_m, alog, dtb, tri):
    bsz, seq, d = x.shape
    tm = TM_IN
    grid = (bsz, seq // tm)
    const = lambda shp: pl.BlockSpec(shp, lambda b, i: (0,) * len(shp))
    row = lambda w, dt: jax.ShapeDtypeStruct((bsz, seq, w), dt)
    return pl.pallas_call(
        _mix_in_kernel,
        out_shape=(row(CONV_WIDTH, BF16), row(GDN_WIDTH, F32), row(GDN_WIDTH, F32), row(GDN_WIDTH, F32),
                   row(GDN_WIDTH, F32), row(LANES, F32),
                   jax.ShapeDtypeStruct((bsz, SUBLANES, seq), F32)),
        grid=grid,
        in_specs=[pl.BlockSpec((1, tm, d), lambda b, i: (b, i, 0)),
                  pl.BlockSpec((1, N_MOD, d), lambda b, i: (b, 0, 0)),
                  const(nw.shape), const(w_ag.shape), const(w_qkv.shape), const(w_z.shape), const(w_ba.shape),
                  const(cw.shape), const(cb.shape), const(gnw.shape), const(gnb.shape), const(gmat.shape),
                  const(scw.shape), const(ones_m.shape), const(alog.shape), const(dtb.shape), const(tri.shape)],
        out_specs=(pl.BlockSpec((1, tm, CONV_WIDTH), lambda b, i: (b, i, 0)),
                   pl.BlockSpec((1, tm, GDN_WIDTH), lambda b, i: (b, i, 0)),
                   pl.BlockSpec((1, tm, GDN_WIDTH), lambda b, i: (b, i, 0)),
                   pl.BlockSpec((1, tm, GDN_WIDTH), lambda b, i: (b, i, 0)),
                   pl.BlockSpec((1, tm, GDN_WIDTH), lambda b, i: (b, i, 0)),
                   pl.BlockSpec((1, tm, LANES), lambda b, i: (b, i, 0)),
                   pl.BlockSpec((1, SUBLANES, tm), lambda b, i: (b, 0, i))),
        scratch_shapes=[pltpu.VMEM((CONV_HALO + tm, CONV_WIDTH), F32),
                        pltpu.VMEM((SC_HALO + tm, 3 * GDN_WIDTH), F32)],
        compiler_params=pltpu.CompilerParams(dimension_semantics=("parallel", "arbitrary"),
                                             vmem_limit_bytes=VMEM_LIMIT),
        name="mix_in",
    )(x, mod, nw, w_ag, w_qkv, w_z, w_ba, cw, cb, gnw, gnb, gmat, scw, ones_m, alog, dtb, tri)


def _gdn_kernel(q_ref, k_ref, v_ref, z_ref, gc_ref, gr_ref, nw_ref, o_ref, state):
    r = q_ref.shape[1]
    nchunk = r // GDN_CHUNK

    @pl.when(pl.program_id(1) == 0)
    def _():
        state[...] = jnp.zeros(state.shape, F32)

    row = lax.broadcasted_iota(jnp.int32, (r, r), 0)
    col = lax.broadcasted_iota(jnp.int32, (r, r), 1)
    log_c = GDN_CHUNK.bit_length() - 1
    same_chunk = (row >> log_c) == (col >> log_c)
    causal = same_chunk & (row >= col)
    strict = same_chunk & (row > col)
    eye = (row == col).astype(F32)

    gc_all = gc_ref[0]
    gr_all = gr_ref[0]
    scale = GDN_HEAD_DIM ** -0.5

    for hd in range(GDN_HEADS):
        ls = slice(hd * GDN_HEAD_DIM, (hd + 1) * GDN_HEAD_DIM)
        q = q_ref[0, :, ls] * scale
        k = k_ref[0, :, ls]
        v = v_ref[0, :, ls]
        gcb = jnp.broadcast_to(gc_all[:, hd:hd + 1], (r, GDN_HEAD_DIM))
        bb = jnp.broadcast_to(gc_all[:, GDN_HEADS + hd:GDN_HEADS + hd + 1], (r, GDN_HEAD_DIM))
        gci = jnp.concatenate([gcb] * (r // GDN_HEAD_DIM), axis=-1)
        gcj = jnp.broadcast_to(gr_all[hd:hd + 1, :], (r, r))
        decay = jnp.where(causal, jnp.exp(jnp.where(causal, gci - gcj, 0.0)), 0.0)

        kb = k * bb
        kk = lax.dot_general(jnp.concatenate([kb, q], axis=0).astype(BF16), k.astype(BF16),
                             (((1,), (1,)), ((), ())), preferred_element_type=F32)
        a_mat = jnp.where(strict, kk[:r] * decay, 0.0)
        qk = kk[r:] * decay

        xinv = eye - jnp.where(((row ^ col) >> 1) == 0, a_mat, 0.0)
        s = 2
        while s < GDN_CHUNK:
            ls2 = s.bit_length()
            m = (((row ^ col) >> ls2) == 0) & ((row & s) != 0) & ((col & s) == 0)
            am = jnp.where(m, a_mat, 0.0)
            xb = xinv.astype(BF16)
            xinv = xinv - _dot(xb, _dot(am, xb))
            s *= 2

        egc = jnp.exp(gcb)
        uw = _dot(xinv, jnp.concatenate([v * bb, kb * egc], axis=-1))
        u_all = uw[:, :GDN_HEAD_DIM]
        w_all = uw[:, GDN_HEAD_DIM:]
        qdec = q * egc

        st = state[hd]
        vnews = []
        qss = []
        for c in range(nchunk):
            rs = slice(c * GDN_CHUNK, (c + 1) * GDN_CHUNK)
            last = (c + 1) * GDN_CHUNK - 1
            wq = jnp.concatenate([w_all[rs], qdec[rs]], axis=0)
            ws = _dot(wq, st)
            v_new = u_all[rs] - ws[:GDN_CHUNK]
            qss.append(ws[GDN_CHUNK:])
            vnews.append(v_new)
            gl = gcb[last:last + 1, :]
            kdec = k[rs] * jnp.exp(gl - gcb[rs])
            cd = jnp.exp(gl)
            st = st * cd + lax.dot_general(kdec.astype(BF16), v_new.astype(BF16),
                                           (((0,), (0,)), ((), ())), preferred_element_type=F32)
        state[hd] = st
        v_new_all = jnp.concatenate(vnews, axis=0)
        o = jnp.concatenate(qss, axis=0) + _dot(qk, v_new_all)

        o = o * lax.rsqrt(jnp.mean(o * o, axis=-1, keepdims=True) + EPS) * nw_ref[...]
        o = o * _silu(z_ref[0, :, ls])
        o_ref[0, :, ls] = o.astype(o_ref.dtype)


def _gdn(q, k, v, z, gc, gr, nw):
    bsz, seq, w = q.shape
    r = GDN_ROWS
    blk = pl.BlockSpec((1, r, w), lambda b, i: (b, i, 0))
    return pl.pallas_call(
        _gdn_kernel,
        out_shape=jax.ShapeDtypeStruct((bsz, seq, w), BF16),
        grid=(bsz, seq // r),
        in_specs=[blk, blk, blk, blk,
                  pl.BlockSpec((1, r, LANES), lambda b, i: (b, i, 0)),
                  pl.BlockSpec((1, SUBLANES, r), lambda b, i: (b, 0, i)),
                  pl.BlockSpec((1, GDN_HEAD_DIM), lambda b, i: (0, 0))],
        out_specs=blk,
        scratch_shapes=[pltpu.VMEM((GDN_HEADS, GDN_HEAD_DIM, GDN_HEAD_DIM), F32)],
        compiler_params=pltpu.CompilerParams(dimension_semantics=("parallel", "arbitrary"),
                                             vmem_limit_bytes=VMEM_LIMIT),
        name="gdn",
    )(q, k, v, z, gc, gr, nw)


def _rms(x, w):
    return x * lax.rsqrt(jnp.mean(x * x, axis=-1, keepdims=True) + EPS) * w


def _out_ffn_kernel(x_ref, oa_ref, ob_ref, mod_ref, wo_ref, nfw_ref, wfi_ref, wfo_ref, nlw_ref, o_ref, *,
                    final_norm):
    x = x_ref[0]
    mix = (jnp.dot(oa_ref[0], wo_ref[0:CONV_WIDTH, :], preferred_element_type=F32)
           + jnp.dot(ob_ref[0], wo_ref[CONV_WIDTH:, :], preferred_element_type=F32))
    x = x + mod_ref[0, 2:3, :] * mix
    h = _rms(x, nfw_ref[...]) * (1.0 + mod_ref[0, 4:5, :]) + mod_ref[0, 3:4, :]
    hb = h.astype(BF16)
    gate = jnp.dot(hb, wfi_ref[:, :D_FF], preferred_element_type=F32)
    up = jnp.dot(hb, wfi_ref[:, D_FF:], preferred_element_type=F32)
    act = (_silu(gate) * up).astype(BF16)
    ffn = jnp.dot(act, wfo_ref[...], preferred_element_type=F32)
    x = x + mod_ref[0, 5:6, :] * ffn
    o_ref[0] = _rms(x, nlw_ref[...]) if final_norm else x


def _out_ffn(x, oa, ob, mod, wo, nfw, wfi, wfo, nlw, final_norm):
    bsz, seq, d = x.shape
    tm = TM_OUT
    once = lambda shp: pl.BlockSpec(shp, lambda b, i: (0,) * len(shp), pipeline_mode=pl.Buffered(1))
    return pl.pallas_call(
        functools.partial(_out_ffn_kernel, final_norm=final_norm),
        out_shape=jax.ShapeDtypeStruct((bsz, seq, d), F32),
        grid=(bsz, seq // tm),
        in_specs=[pl.BlockSpec((1, tm, d), lambda b, i: (b, i, 0)),
                  pl.BlockSpec((1, tm, CONV_WIDTH), lambda b, i: (b, i, 0)),
                  pl.BlockSpec((1, tm, GDN_WIDTH), lambda b, i: (b, i, 0)),
                  pl.BlockSpec((1, N_MOD, d), lambda b, i: (b, 0, 0)),
                  once(wo.shape), once(nfw.shape), once(wfi.shape), once(wfo.shape), once(nlw.shape)],
        out_specs=pl.BlockSpec((1, tm, d), lambda b, i: (b, i, 0)),
        compiler_params=pltpu.CompilerParams(dimension_semantics=("parallel", "parallel"),
                                             vmem_limit_bytes=VMEM_LIMIT),
        name="out_ffn",
    )(x, oa, ob, mod, wo, nfw, wfi, wfo, nlw)


def _block_diag_const(n, blk, val):
    idx = np.arange(n) // blk
    return jnp.asarray((idx[:, None] == idx[None, :]).astype(np.float32) * val)


def _chunk_tri_const(n, blk):
    i = np.arange(n)
    m = ((i[:, None] // blk) == (i[None, :] // blk)) & (i[:, None] >= i[None, :])
    return jnp.asarray(m.astype(np.float32))


def _pad_lanes(a, n):
    return jnp.pad(a, [(0, 0)] * (a.ndim - 1) + [(0, n - a.shape[-1])])


def kernel(x, c, w_ada, b_ada, norm_mix_w, w_in, conv_w, conv_b, conv_gn_w, conv_gn_b, gdn_conv_w, gdn_a_log, gdn_dt_bias, gdn_norm_w, w_out, norm_ffn_w, w_ffn_in, w_ffn_out, norm_final_w):
    depth = w_ada.shape[0]
    bsz = x.shape[0]
    gmat = _block_diag_const(MXU_DIM, CONV_WIDTH // CONV_GROUPS, 1.0 / (CONV_WIDTH // CONV_GROUPS)).astype(BF16)
    ones_m = _block_diag_const(MXU_DIM, GDN_HEAD_DIM, 1.0).astype(BF16)
    tri = _chunk_tri_const(TM_IN, GDN_CHUNK)
    for layer in range(depth):
        mod = _adaln(c, w_ada[layer], b_ada[layer]).reshape(bsz, N_MOD, D_MODEL)
        wl = w_in[layer]
        o1 = 2 * CONV_WIDTH
        o2 = o1 + 3 * GDN_WIDTH
        o3 = o2 + GDN_WIDTH
        w_ag = wl[:, :o1].astype(BF16)
        w_qkv = wl[:, o1:o2].astype(BF16)
        w_z = wl[:, o2:o3].astype(BF16)
        w_ba = jnp.concatenate([_pad_lanes(wl[:, o3:o3 + GDN_HEADS], LANES),
                                _pad_lanes(wl[:, o3 + GDN_HEADS:], LANES)], axis=-1).astype(BF16)
        oa, qn, kn, vv, z, gc, gr = _mix_in(
            x, mod, norm_mix_w[layer][None, :], w_ag, w_qkv, w_z, w_ba,
            _pad_lanes(conv_w[layer].T, CONV_HALO).T, conv_b[layer][None, :],
            conv_gn_w[layer][None, :], conv_gn_b[layer][None, :], gmat,
            gdn_conv_w[layer], ones_m,
            _pad_lanes(gdn_a_log[layer][None, :], LANES), _pad_lanes(gdn_dt_bias[layer][None, :], LANES), tri)
        ob = _gdn(qn, kn, vv, z, gc, gr, gdn_norm_w[layer][None, :])
        x = _out_ffn(x, oa, ob, mod, w_out[layer].astype(BF16), norm_ffn_w[layer][None, :],
                     w_ffn_in[layer].astype(BF16), w_ffn_out[layer].astype(BF16), norm_final_w[None, :],
                     final_norm=layer == depth - 1)
    return x
```

```python
import functools

import jax
import jax.numpy as jnp
import numpy as np
from jax import lax
from jax.experimental import pallas as pl
from jax.experimental.pallas import tpu as pltpu

F32 = jnp.float32
BF16 = jnp.bfloat16

D_MODEL = 1024
CONV_WIDTH = 512
CONV_GROUPS = 8
CONV_KERNEL = 31
GDN_WIDTH = 512
GDN_HEAD_DIM = 128
GDN_HEADS = 4
GDN_SHORT_CONV = 4
GDN_CHUNK = 64
D_FF = 2816
N_MOD = 6
EPS = 1e-6

LANES = 128
SUBLANES = 8
MXU_DIM = 256

CONV_HALO = 32
SC_HALO = SUBLANES
TM_IN = 256
GDN_ROWS = 256
TM_OUT = 512
VMEM_LIMIT = 56 * 1024 * 1024


def _silu(v):
    return v * jax.nn.sigmoid(v)


def _dot(a, b):
    return jnp.dot(a.astype(BF16), b.astype(BF16), preferred_element_type=F32)


def _adaln_kernel(c_ref, w_ref, b_ref, o_ref):
    c = c_ref[...]
    o_ref[...] = jnp.dot(_silu(c), w_ref[...], preferred_element_type=F32,
                         precision=lax.Precision.HIGHEST) + b_ref[...]


def _adaln(c, w, b):
    bsz, d = c.shape
    n = w.shape[1]
    tn = 1024
    return pl.pallas_call(
        _adaln_kernel,
        out_shape=jax.ShapeDtypeStruct((bsz, n), F32),
        grid=(n // tn,),
        in_specs=[pl.BlockSpec((bsz, d), lambda j: (0, 0)),
                  pl.BlockSpec((d, tn), lambda j: (0, j)),
                  pl.BlockSpec((1, tn), lambda j: (0, j))],
        out_specs=pl.BlockSpec((bsz, tn), lambda j: (0, j)),
        compiler_params=pltpu.CompilerParams(dimension_semantics=("arbitrary",),
                                             vmem_limit_bytes=VMEM_LIMIT),
        name="adaln",
    )(c, w, b.reshape(1, n))


def _mix_in_kernel(x_ref, mod_ref, nw_ref, w_ag_ref, w_qkv_ref, w_z_ref, w_ba_ref,
                   cw_ref, cb_ref, gnw_ref, gnb_ref, gmat_ref, scw_ref, ones_ref,
                   alog_ref, dtb_ref, tri_ref,
                   oa_ref, q_ref, k_ref, v_ref, z_ref, gc_ref, gr_ref,
                   ubuf, qbuf):
    i = pl.program_id(1)
    tm = x_ref.shape[1]

    @pl.when(i == 0)
    def _():
        ubuf[0:CONV_HALO, :] = jnp.zeros((CONV_HALO, CONV_WIDTH), F32)
        qbuf[0:SC_HALO, :] = jnp.zeros((SC_HALO, 3 * GDN_WIDTH), F32)

    x = x_ref[0]
    ms = jnp.mean(x * x, axis=-1, keepdims=True)
    h = x * lax.rsqrt(ms + EPS) * nw_ref[...]
    h = h * (1.0 + mod_ref[0, 1:2, :]) + mod_ref[0, 0:1, :]
    hb = h.astype(BF16)

    ag = jnp.dot(hb, w_ag_ref[...], preferred_element_type=F32)
    u = ag[:, :CONV_WIDTH] * jax.nn.sigmoid(ag[:, CONV_WIDTH:])
    ubuf[CONV_HALO:CONV_HALO + tm, :] = u
    base = CONV_HALO - (CONV_KERNEL - 1)
    for hp in range(CONV_WIDTH // MXU_DIM):
        half = slice(hp * MXU_DIM, (hp + 1) * MXU_DIM)
        ys = []
        for j in range(hp * MXU_DIM // LANES, (hp + 1) * MXU_DIM // LANES):
            ls = slice(j * LANES, (j + 1) * LANES)
            acc = cb_ref[:, ls] + cw_ref[0:1, ls] * ubuf[base:base + tm, ls]
            for t in range(1, CONV_KERNEL):
                acc = acc + cw_ref[t:t + 1, ls] * ubuf[base + t:base + t + tm, ls]
            ys.append(acc)
        y = jnp.concatenate(ys, axis=-1)
        mu = _dot(y, gmat_ref[...])
        d = y - mu
        var = _dot(d * d, gmat_ref[...])
        un = d * lax.rsqrt(var + EPS) * gnw_ref[:, half] + gnb_ref[:, half]
        oa_ref[0, :, half] = _silu(un).astype(oa_ref.dtype)
    ubuf[0:CONV_HALO, :] = ubuf[tm:tm + CONV_HALO, :]

    qkv = jnp.dot(hb, w_qkv_ref[...], preferred_element_type=F32)
    qbuf[SC_HALO:SC_HALO + tm, :] = qkv
    sbase = SC_HALO - (GDN_SHORT_CONV - 1)
    outs = (q_ref, k_ref, v_ref)
    for part in range(3):
        for hp in range(GDN_WIDTH // MXU_DIM):
            ls = slice(part * GDN_WIDTH + hp * MXU_DIM, part * GDN_WIDTH + (hp + 1) * MXU_DIM)
            acc = scw_ref[0:1, ls] * qbuf[sbase:sbase + tm, ls]
            for t in range(1, GDN_SHORT_CONV):
                acc = acc + scw_ref[t:t + 1, ls] * qbuf[sbase + t:sbase + t + tm, ls]
            a = _silu(acc)
            if part < 2:
                ss = _dot(a * a, ones_ref[...])
                a = a * lax.rsqrt(ss + EPS)
            outs[part][0, :, hp * MXU_DIM:(hp + 1) * MXU_DIM] = a
    qbuf[0:SC_HALO, :] = qbuf[tm:tm + SC_HALO, :]

    z_ref[0] = jnp.dot(hb, w_z_ref[...], preferred_element_type=F32)

    ba = jnp.dot(hb, w_ba_ref[...], preferred_element_type=F32)
    beta = jax.nn.sigmoid(ba[:, :LANES])
    al = ba[:, LANES:] + dtb_ref[...]
    sp = jnp.maximum(al, 0.0) + jnp.log1p(jnp.exp(-jnp.abs(al)))
    g = -jnp.exp(alog_ref[...]) * sp
    gcum = jnp.dot(tri_ref[...], g, preferred_element_type=F32, precision=lax.Precision.HIGHEST)
    lane = lax.broadcasted_iota(jnp.int32, (tm, LANES), 1)
    packed = jnp.where(lane < GDN_HEADS, gcum, pltpu.roll(beta, GDN_HEADS, 1))
    gc_ref[0] = packed
    gr_ref[0] = packed.T[0:SUBLANES, :]


def _mix_in(x, mod, nw, w_ag, w_qkv, w_z, w_ba, cw, cb, gnw, gnb, gmat, scw, ones_m, alog, dtb, tri):
    bsz, seq, d = x.shape
    tm = TM_IN
    grid = (bsz, seq // tm)
    const = lambda shp: pl.BlockSpec(shp, lambda b, i: (0,) * len(shp))
    row = lambda w, dt: jax.ShapeDtypeStruct((bsz, seq, w), dt)
    return pl.pallas_call(
        _mix_in_kernel,
        out_shape=(row(CONV_WIDTH, BF16), row(GDN_WIDTH, F32), row(GDN_WIDTH, F32), row(GDN_WIDTH, F32),
                   row(GDN_WIDTH, F32), row(LANES, F32),
                   jax.ShapeDtypeStruct((bsz, SUBLANES, seq), F32)),
        grid=grid,
        in_specs=[pl.BlockSpec((1, tm, d), lambda b, i: (b, i, 0)),
                  pl.BlockSpec((1, N_MOD, d), lambda b, i: (b, 0, 0)),
                  const(nw.shape), const(w_ag.shape), const(w_qkv.shape), const(w_z.shape), const(w_ba.shape),
                  const(cw.shape), const(cb.shape), const(gnw.shape), const(gnb.shape), const(gmat.shape),
                  const(scw.shape), const(ones_m.shape), const(alog.shape), const(dtb.shape), const(tri.shape)],
        out_specs=(pl.BlockSpec((1, tm, CONV_WIDTH), lambda b, i: (b, i, 0)),
                   pl.BlockSpec((1, tm, GDN_WIDTH), lambda b, i: (b, i, 0)),
                   pl.BlockSpec((1, tm, GDN_WIDTH), lambda b, i: (b, i, 0)),
                   pl.BlockSpec((1, tm, GDN_WIDTH), lambda b, i: (b, i, 0)),
                   pl.BlockSpec((1, tm, GDN_WIDTH), lambda b, i: (b, i, 0)),
                   pl.BlockSpec((1, tm, LANES), lambda b, i: (b, i, 0)),
                   pl.BlockSpec((1, SUBLANES, tm), lambda b, i: (b, 0, i))),
        scratch_shapes=[pltpu.VMEM((CONV_HALO + tm, CONV_WIDTH), F32),
                        pltpu.VMEM((SC_HALO + tm, 3 * GDN_WIDTH), F32)],
        compiler_params=pltpu.CompilerParams(dimension_semantics=("parallel", "arbitrary"),
                                             vmem_limit_bytes=VMEM_LIMIT),
        name="mix_in",
    )(x, mod, nw, w_ag, w_qkv, w_z, w_ba, cw, cb, gnw, gnb, gmat, scw, ones_m, alog, dtb, tri)


def _gdn_kernel(q_ref, k_ref, v_ref, z_ref, gc_ref, gr_ref, nw_ref, o_ref, state):
    r = q_ref.shape[1]
    nchunk = r // GDN_CHUNK

    @pl.when(pl.program_id(1) == 0)
    def _():
        state[...] = jnp.zeros(state.shape, F32)

    row = lax.broadcasted_iota(jnp.int32, (r, r), 0)
    col = lax.broadcasted_iota(jnp.int32, (r, r), 1)
    log_c = GDN_CHUNK.bit_length() - 1
    same_chunk = (row >> log_c) == (col >> log_c)
    causal = same_chunk & (row >= col)
    strict = same_chunk & (row > col)
    eye = (row == col).astype(F32)

    gc_all = gc_ref[0]
    gr_all = gr_ref[0]
    scale = GDN_HEAD_DIM ** -0.5

    heads = range(GDN_HEADS)
    lanes = [slice(hd * GDN_HEAD_DIM, (hd + 1) * GDN_HEAD_DIM) for hd in heads]
    q = [q_ref[0, :, ls] * scale for ls in lanes]
    k = [k_ref[0, :, ls] for ls in lanes]
    v = [v_ref[0, :, ls] for ls in lanes]
    gcb = [jnp.broadcast_to(gc_all[:, hd:hd + 1], (r, GDN_HEAD_DIM)) for hd in heads]
    bb = [jnp.broadcast_to(gc_all[:, GDN_HEADS + hd:GDN_HEADS + hd + 1], (r, GDN_HEAD_DIM)) for hd in heads]
    decay = []
    for hd in heads:
        gci = jnp.concatenate([gcb[hd]] * (r // GDN_HEAD_DIM), axis=-1)
        gcj = jnp.broadcast_to(gr_all[hd:hd + 1, :], (r, r))
        decay.append(jnp.where(causal, jnp.exp(jnp.where(causal, gci - gcj, 0.0)), 0.0))

    kb = [k[hd] * bb[hd] for hd in heads]
    kk = [lax.dot_general(jnp.concatenate([kb[hd], q[hd]], axis=0).astype(BF16), k[hd].astype(BF16),
                          (((1,), (1,)), ((), ())), preferred_element_type=F32) for hd in heads]
    a_mat = [jnp.where(strict, kk[hd][:r] * decay[hd], 0.0) for hd in heads]
    qk = [kk[hd][r:] * decay[hd] for hd in heads]

    pair = ((row ^ col) >> 1) == 0
    xinv = [eye - jnp.where(pair, a_mat[hd], 0.0) for hd in heads]
    s = 2
    while s < GDN_CHUNK:
        ls2 = s.bit_length()
        m = (((row ^ col) >> ls2) == 0) & ((row & s) != 0) & ((col & s) == 0)
        xb = [xinv[hd].astype(BF16) for hd in heads]
        ax = [_dot(jnp.where(m, a_mat[hd], 0.0), xb[hd]) for hd in heads]
        xinv = [xinv[hd] - _dot(xb[hd], ax[hd]) for hd in heads]
        s *= 2

    egc = [jnp.exp(gcb[hd]) for hd in heads]
    uw = [_dot(xinv[hd], jnp.concatenate([v[hd] * bb[hd], kb[hd] * egc[hd]], axis=-1)) for hd in heads]
    u_all = [uw[hd][:, :GDN_HEAD_DIM] for hd in heads]
    w_all = [uw[hd][:, GDN_HEAD_DIM:] for hd in heads]
    qdec = [q[hd] * egc[hd] for hd in heads]

    st = [state[hd] for hd in heads]
    vnews = [[] for _ in heads]
    qss = [[] for _ in heads]
    for c in range(nchunk):
        rs = slice(c * GDN_CHUNK, (c + 1) * GDN_CHUNK)
        last = (c + 1) * GDN_CHUNK - 1
        ws = [_dot(jnp.concatenate([w_all[hd][rs], qdec[hd][rs]], axis=0), st[hd]) for hd in heads]
        for hd in heads:
            v_new = u_all[hd][rs] - ws[hd][:GDN_CHUNK]
            qss[hd].append(ws[hd][GDN_CHUNK:])
            vnews[hd].append(v_new)
            gl = gcb[hd][last:last + 1, :]
            kdec = k[hd][rs] * jnp.exp(gl - gcb[hd][rs])
            st[hd] = st[hd] * jnp.exp(gl) + lax.dot_general(
                kdec.astype(BF16), v_new.astype(BF16), (((0,), (0,)), ((), ())), preferred_element_type=F32)
    for hd in heads:
        state[hd] = st[hd]
    for hd in heads:
        o = jnp.concatenate(qss[hd], axis=0) + _dot(qk[hd], jnp.concatenate(vnews[hd], axis=0))
        o = o * lax.rsqrt(jnp.mean(o * o, axis=-1, keepdims=True) + EPS) * nw_ref[...]
        o = o * _silu(z_ref[0, :, lanes[hd]])
        o_ref[0, :, lanes[hd]] = o.astype(o_ref.dtype)


def _gdn(q, k, v, z, gc, gr, nw):
    bsz, seq, w = q.shape
    r = GDN_ROWS
    blk = pl.BlockSpec((1, r, w), lambda b, i: (b, i, 0))
    return pl.pallas_call(
        _gdn_kernel,
        out_shape=jax.ShapeDtypeStruct((bsz, seq, w), BF16),
        grid=(bsz, seq // r),
        in_specs=[blk, blk, blk, blk,
                  pl.BlockSpec((1, r, LANES), lambda b, i: (b, i, 0)),
                  pl.BlockSpec((1, SUBLANES, r), lambda b, i: (b, 0, i)),
                  pl.BlockSpec((1, GDN_HEAD_DIM), lambda b, i: (0, 0))],
        out_specs=blk,
        scratch_shapes=[pltpu.VMEM((GDN_HEADS, GDN_HEAD_DIM, GDN_HEAD_DIM), F32)],
        compiler_params=pltpu.CompilerParams(dimension_semantics=("parallel", "arbitrary"),
                                             vmem_limit_bytes=VMEM_LIMIT),
        name="gdn",
    )(q, k, v, z, gc, gr, nw)


def _rms(x, w):
    return x * lax.rsqrt(jnp.mean(x * x, axis=-1, keepdims=True) + EPS) * w


def _out_ffn_kernel(x_ref, oa_ref, ob_ref, mod_ref, wo_ref, nfw_ref, wfi_ref, wfo_ref, nlw_ref, o_ref, *,
                    final_norm):
    x = x_ref[0]
    mix = (jnp.dot(oa_ref[0], wo_ref[0:CONV_WIDTH, :], preferred_element_type=F32)
           + jnp.dot(ob_ref[0], wo_ref[CONV_WIDTH:, :], preferred_element_type=F32))
    x = x + mod_ref[0, 2:3, :] * mix
    h = _rms(x, nfw_ref[...]) * (1.0 + mod_ref[0, 4:5, :]) + mod_ref[0, 3:4, :]
    hb = h.astype(BF16)
    gate = jnp.dot(hb, wfi_ref[:, :D_FF], preferred_element_type=F32)
    up = jnp.dot(hb, wfi_ref[:, D_FF:], preferred_element_type=F32)
    act = (_silu(gate) * up).astype(BF16)
    ffn = jnp.dot(act, wfo_ref[...], preferred_element_type=F32)
    x = x + mod_ref[0, 5:6, :] * ffn
    o_ref[0] = _rms(x, nlw_ref[...]) if final_norm else x


def _out_ffn(x, oa, ob, mod, wo, nfw, wfi, wfo, nlw, final_norm):
    bsz, seq, d = x.shape
    tm = TM_OUT
    once = lambda shp: pl.BlockSpec(shp, lambda b, i: (0,) * len(shp), pipeline_mode=pl.Buffered(1))
    return pl.pallas_call(
        functools.partial(_out_ffn_kernel, final_norm=final_norm),
        out_shape=jax.ShapeDtypeStruct((bsz, seq, d), F32),
        grid=(bsz, seq // tm),
        in_specs=[pl.BlockSpec((1, tm, d), lambda b, i: (b, i, 0)),
                  pl.BlockSpec((1, tm, CONV_WIDTH), lambda b, i: (b, i, 0)),
                  pl.BlockSpec((1, tm, GDN_WIDTH), lambda b, i: (b, i, 0)),
                  pl.BlockSpec((1, N_MOD, d), lambda b, i: (b, 0, 0)),
                  once(wo.shape), once(nfw.shape), once(wfi.shape), once(wfo.shape), once(nlw.shape)],
        out_specs=pl.BlockSpec((1, tm, d), lambda b, i: (b, i, 0)),
        compiler_params=pltpu.CompilerParams(dimension_semantics=("parallel", "parallel"),
                                             vmem_limit_bytes=VMEM_LIMIT),
        name="out_ffn",
    )(x, oa, ob, mod, wo, nfw, wfi, wfo, nlw)


def _block_diag_const(n, blk, val):
    idx = np.arange(n) // blk
    return jnp.asarray((idx[:, None] == idx[None, :]).astype(np.float32) * val)


def _chunk_tri_const(n, blk):
    i = np.arange(n)
    m = ((i[:, None] // blk) == (i[None, :] // blk)) & (i[:, None] >= i[None, :])
    return jnp.asarray(m.astype(np.float32))


def _pad_lanes(a, n):
    return jnp.pad(a, [(0, 0)] * (a.ndim - 1) + [(0, n - a.shape[-1])])


def kernel(x, c, w_ada, b_ada, norm_mix_w, w_in, conv_w, conv_b, conv_gn_w, conv_gn_b, gdn_conv_w, gdn_a_log, gdn_dt_bias, gdn_norm_w, w_out, norm_ffn_w, w_ffn_in, w_ffn_out, norm_final_w):
    depth = w_ada.shape[0]
    bsz = x.shape[0]
    gmat = _block_diag_const(MXU_DIM, CONV_WIDTH // CONV_GROUPS, 1.0 / (CONV_WIDTH // CONV_GROUPS)).astype(BF16)
    ones_m = _block_diag_const(MXU_DIM, GDN_HEAD_DIM, 1.0).astype(BF16)
    tri = _chunk_tri_const(TM_IN, GDN_CHUNK)
    for layer in range(depth):
        mod = _adaln(c, w_ada[layer], b_ada[layer]).reshape(bsz, N_MOD, D_MODEL)
        wl = w_in[layer]
        o1 = 2 * CONV_WIDTH
        o2 = o1 + 3 * GDN_WIDTH
        o3 = o2 + GDN_WIDTH
        w_ag = wl[:, :o1].astype(BF16)
        w_qkv = wl[:, o1:o2].astype(BF16)
        w_z = wl[:, o2:o3].astype(BF16)
        w_ba = jnp.concatenate([_pad_lanes(wl[:, o3:o3 + GDN_HEADS], LANES),
                                _pad_lanes(wl[:, o3 + GDN_HEADS:], LANES)], axis=-1).astype(BF16)
        oa, qn, kn, vv, z, gc, gr = _mix_in(
            x, mod, norm_mix_w[layer][None, :], w_ag, w_qkv, w_z, w_ba,
            _pad_lanes(conv_w[layer].T, CONV_HALO).T, conv_b[layer][None, :],
            conv_gn_w[layer][None, :], conv_gn_b[layer][None, :], gmat,
            gdn_conv_w[layer], ones_m,
            _pad_lanes(gdn_a_log[layer][None, :], LANES), _pad_lanes(gdn_dt_bias[layer][None, :], LANES), tri)
        ob = _gdn(qn, kn, vv, z, gc, gr, gdn_norm_w[layer][None, :])
        x = _out_ffn(x, oa, ob, mod, w_out[layer].astype(BF16), norm_ffn_w[layer][None, :],
                     w_ffn_in[layer].astype(BF16), w_ffn_out[layer].astype(BF16), norm_final_w[None, :],
                     final_norm=layer == depth - 1)
    return x
```

```python
import functools

import jax
import jax.numpy as jnp
import numpy as np
from jax import lax
from jax.experimental import pallas as pl
from jax.experimental.pallas import tpu as pltpu

F32 = jnp.float32
BF16 = jnp.bfloat16

D_MODEL = 1024
CONV_WIDTH = 512
CONV_GROUPS = 8
CONV_KERNEL = 31
GDN_WIDTH = 512
GDN_HEAD_DIM = 128
GDN_HEADS = 4
GDN_SHORT_CONV = 4
GDN_CHUNK = 64
D_FF = 2816
N_MOD = 6
EPS = 1e-6

LANES = 128
SUBLANES = 8
MXU_DIM = 256

CONV_HALO = 32
SC_HALO = SUBLANES
TM_IN = 256
GDN_ROWS = 256
TM_OUT = 512
VMEM_LIMIT = 56 * 1024 * 1024


def _silu(v):
    return v * jax.nn.sigmoid(v)


def _dot(a, b):
    return jnp.dot(a.astype(BF16), b.astype(BF16), preferred_element_type=F32)


def _adaln_kernel(c_ref, w_ref, b_ref, o_ref):
    c = c_ref[...]
    o_ref[...] = jnp.dot(_silu(c), w_ref[...], preferred_element_type=F32,
                         precision=lax.Precision.HIGHEST) + b_ref[...]


def _adaln(c, w, b):
    bsz, d = c.shape
    n = w.shape[1]
    tn = 1024
    return pl.pallas_call(
        _adaln_kernel,
        out_shape=jax.ShapeDtypeStruct((bsz, n), F32),
        grid=(n // tn,),
        in_specs=[pl.BlockSpec((bsz, d), lambda j: (0, 0)),
                  pl.BlockSpec((d, tn), lambda j: (0, j)),
                  pl.BlockSpec((1, tn), lambda j: (0, j))],
        out_specs=pl.BlockSpec((bsz, tn), lambda j: (0, j)),
        compiler_params=pltpu.CompilerParams(dimension_semantics=("arbitrary",),
                                             vmem_limit_bytes=VMEM_LIMIT),
        name="adaln",
    )(c, w, b.reshape(1, n))


def _mix_in_kernel(x_ref, mod_ref, nw_ref, w_ag_ref, w_qkv_ref, w_z_ref, w_ba_ref,
                   cw_ref, cb_ref, gnw_ref, gnb_ref, gmat_ref, scw_ref, ones_ref,
                   alog_ref, dtb_ref, tri_ref,
                   oa_ref, q_ref, k_ref, v_ref, z_ref, gc_ref, gr_ref,
                   ubuf, qbuf):
    i = pl.program_id(1)
    tm = x_ref.shape[1]

    @pl.when(i == 0)
    def _():
        ubuf[0:CONV_HALO, :] = jnp.zeros((CONV_HALO, CONV_WIDTH), F32)
        qbuf[0:SC_HALO, :] = jnp.zeros((SC_HALO, 3 * GDN_WIDTH), F32)

    x = x_ref[0]
    ms = jnp.mean(x * x, axis=-1, keepdims=True)
    h = x * lax.rsqrt(ms + EPS) * nw_ref[...]
    h = h * (1.0 + mod_ref[0, 1:2, :]) + mod_ref[0, 0:1, :]
    hb = h.astype(BF16)

    ag = jnp.dot(hb, w_ag_ref[...], preferred_element_type=F32)
    u = ag[:, :CONV_WIDTH] * jax.nn.sigmoid(ag[:, CONV_WIDTH:])
    ubuf[CONV_HALO:CONV_HALO + tm, :] = u
    base = CONV_HALO - (CONV_KERNEL - 1)
    for hp in range(CONV_WIDTH // MXU_DIM):
        half = slice(hp * MXU_DIM, (hp + 1) * MXU_DIM)
        ys = []
        for j in range(hp * MXU_DIM // LANES, (hp + 1) * MXU_DIM // LANES):
            ls = slice(j * LANES, (j + 1) * LANES)
            full = ubuf[:, ls]
            rows = full.shape[0]
            acc = jnp.zeros((tm, LANES), F32) + cb_ref[:, ls]
            for b in range(SUBLANES):
                win = pltpu.roll(full, (rows - base - b) % rows, 0)
                for a in range(-(-(CONV_KERNEL - b) // SUBLANES)):
                    t = SUBLANES * a + b
                    acc = acc + cw_ref[t:t + 1, ls] * win[a * SUBLANES:a * SUBLANES + tm]
            ys.append(acc)
        y = jnp.concatenate(ys, axis=-1)
        mu = _dot(y, gmat_ref[...])
        d = y - mu
        var = _dot(d * d, gmat_ref[...])
        un = d * lax.rsqrt(var + EPS) * gnw_ref[:, half] + gnb_ref[:, half]
        oa_ref[0, :, half] = _silu(un).astype(oa_ref.dtype)
    ubuf[0:CONV_HALO, :] = ubuf[tm:tm + CONV_HALO, :]

    qkv = jnp.dot(hb, w_qkv_ref[...], preferred_element_type=F32)
    qbuf[SC_HALO:SC_HALO + tm, :] = qkv
    sbase = SC_HALO - (GDN_SHORT_CONV - 1)
    outs = (q_ref, k_ref, v_ref)
    for part in range(3):
        for hp in range(GDN_WIDTH // MXU_DIM):
            ls = slice(part * GDN_WIDTH + hp * MXU_DIM, part * GDN_WIDTH + (hp + 1) * MXU_DIM)
            acc = scw_ref[0:1, ls] * qbuf[sbase:sbase + tm, ls]
            for t in range(1, GDN_SHORT_CONV):
                acc = acc + scw_ref[t:t + 1, ls] * qbuf[sbase + t:sbase + t + tm, ls]
            a = _silu(acc)
            if part < 2:
                ss = _dot(a * a, ones_ref[...])
                a = a * lax.rsqrt(ss + EPS)
            outs[part][0, :, hp * MXU_DIM:(hp + 1) * MXU_DIM] = a
    qbuf[0:SC_HALO, :] = qbuf[tm:tm + SC_HALO, :]

    z_ref[0] = jnp.dot(hb, w_z_ref[...], preferred_element_type=F32)

    ba = jnp.dot(hb, w_ba_ref[...], preferred_element_type=F32)
    beta = jax.nn.sigmoid(ba[:, :LANES])
    al = ba[:, LANES:] + dtb_ref[...]
    sp = jnp.maximum(al, 0.0) + jnp.log1p(jnp.exp(-jnp.abs(al)))
    g = -jnp.exp(alog_ref[...]) * sp
    gcum = jnp.dot(tri_ref[...], g, preferred_element_type=F32, precision=lax.Precision.HIGHEST)
    lane = lax.broadcasted_iota(jnp.int32, (tm, LANES), 1)
    packed = jnp.where(lane < GDN_HEADS, gcum, pltpu.roll(beta, GDN_HEADS, 1))
    gc_ref[0] = packed
    gr_ref[0] = packed.T[0:SUBLANES, :]


def _mix_in(x, mod, nw, w_ag, w_qkv, w_z, w_ba, cw, cb, gnw, gnb, gmat, scw, ones_m, alog, dtb, tri):
    bsz, seq, d = x.shape
    tm = TM_IN
    grid = (bsz, seq // tm)
    const = lambda shp: pl.BlockSpec(shp, lambda b, i: (0,) * len(shp))
    row = lambda w, dt: jax.ShapeDtypeStruct((bsz, seq, w), dt)
    return pl.pallas_call(
        _mix_in_kernel,
        out_shape=(row(CONV_WIDTH, BF16), row(GDN_WIDTH, F32), row(GDN_WIDTH, F32), row(GDN_WIDTH, F32),
                   row(GDN_WIDTH, F32), row(LANES, F32),
                   jax.ShapeDtypeStruct((bsz, SUBLANES, seq), F32)),
        grid=grid,
        in_specs=[pl.BlockSpec((1, tm, d), lambda b, i: (b, i, 0)),
                  pl.BlockSpec((1, N_MOD, d), lambda b, i: (b, 0, 0)),
                  const(nw.shape), const(w_ag.shape), const(w_qkv.shape), const(w_z.shape), const(w_ba.shape),
                  const(cw.shape), const(cb.shape), const(gnw.shape), const(gnb.shape), const(gmat.shape),
                  const(scw.shape), const(ones_m.shape), const(alog.shape), const(dtb.shape), const(tri.shape)],
        out_specs=(pl.BlockSpec((1, tm, CONV_WIDTH), lambda b, i: (b, i, 0)),
                   pl.BlockSpec((1, tm, GDN_WIDTH), lambda b, i: (b, i, 0)),
                   pl.BlockSpec((1, tm, GDN_WIDTH), lambda b, i: (b, i, 0)),
                   pl.BlockSpec((1, tm, GDN_WIDTH), lambda b, i: (b, i, 0)),
                   pl.BlockSpec((1, tm, GDN_WIDTH), lambda b, i: (b, i, 0)),
                   pl.BlockSpec((1, tm, LANES), lambda b, i: (b, i, 0)),
                   pl.BlockSpec((1, SUBLANES, tm), lambda b, i: (b, 0, i))),
        scratch_shapes=[pltpu.VMEM((CONV_HALO + tm, CONV_WIDTH), F32),
                        pltpu.VMEM((SC_HALO + tm, 3 * GDN_WIDTH), F32)],
        compiler_params=pltpu.CompilerParams(dimension_semantics=("parallel", "arbitrary"),
                                             vmem_limit_bytes=VMEM_LIMIT),
        name="mix_in",
    )(x, mod, nw, w_ag, w_qkv, w_z, w_ba, cw, cb, gnw, gnb, gmat, scw, ones_m, alog, dtb, tri)


def _gdn_kernel(q_ref, k_ref, v_ref, z_ref, gc_ref, gr_ref, nw_ref, o_ref, state):
    r = q_ref.shape[1]
    nchunk = r // GDN_CHUNK

    @pl.when(pl.program_id(1) == 0)
    def _():
        state[...] = jnp.zeros(state.shape, F32)

    row = lax.broadcasted_iota(jnp.int32, (r, r), 0)
    col = lax.broadcasted_iota(jnp.int32, (r, r), 1)
    log_c = GDN_CHUNK.bit_length() - 1
    same_chunk = (row >> log_c) == (col >> log_c)
    causal = same_chunk & (row >= col)
    strict = same_chunk & (row > col)
    eye = (row == col).astype(F32)

    gc_all = gc_ref[0]
    gr_all = gr_ref[0]
    scale = GDN_HEAD_DIM ** -0.5

    heads = range(GDN_HEADS)
    lanes = [slice(hd * GDN_HEAD_DIM, (hd + 1) * GDN_HEAD_DIM) for hd in heads]
    q = [q_ref[0, :, ls] * scale for ls in lanes]
    k = [k_ref[0, :, ls] for ls in lanes]
    v = [v_ref[0, :, ls] for ls in lanes]
    gcb = [jnp.broadcast_to(gc_all[:, hd:hd + 1], (r, GDN_HEAD_DIM)) for hd in heads]
    bb = [jnp.broadcast_to(gc_all[:, GDN_HEADS + hd:GDN_HEADS + hd + 1], (r, GDN_HEAD_DIM)) for hd in heads]
    decay = []
    for hd in heads:
        gci = jnp.concatenate([gcb[hd]] * (r // GDN_HEAD_DIM), axis=-1)
        gcj = jnp.broadcast_to(gr_all[hd:hd + 1, :], (r, r))
        decay.append(jnp.where(causal, jnp.exp(jnp.where(causal, gci - gcj, 0.0)), 0.0))

    kb = [k[hd] * bb[hd] for hd in heads]
    kk = [lax.dot_general(jnp.concatenate([kb[hd], q[hd]], axis=0).astype(BF16), k[hd].astype(BF16),
                          (((1,), (1,)), ((), ())), preferred_element_type=F32) for hd in heads]
    a_mat = [jnp.where(strict, kk[hd][:r] * decay[hd], 0.0) for hd in heads]
    qk = [kk[hd][r:] * decay[hd] for hd in heads]

    pair = ((row ^ col) >> 1) == 0
    xinv = [eye - jnp.where(pair, a_mat[hd], 0.0) for hd in heads]
    s = 2
    while s < GDN_CHUNK:
        ls2 = s.bit_length()
        m = (((row ^ col) >> ls2) == 0) & ((row & s) != 0) & ((col & s) == 0)
        xb = [xinv[hd].astype(BF16) for hd in heads]
        ax = [_dot(jnp.where(m, a_mat[hd], 0.0), xb[hd]) for hd in heads]
        xinv = [xinv[hd] - _dot(xb[hd], ax[hd]) for hd in heads]
        s *= 2

    egc = [jnp.exp(gcb[hd]) for hd in heads]
    uw = [_dot(xinv[hd], jnp.concatenate([v[hd] * bb[hd], kb[hd] * egc[hd]], axis=-1)) for hd in heads]
    u_all = [uw[hd][:, :GDN_HEAD_DIM] for hd in heads]
    w_all = [uw[hd][:, GDN_HEAD_DIM:] for hd in heads]
    qdec = [q[hd] * egc[hd] for hd in heads]

    st = [state[hd] for hd in heads]
    vnews = [[] for _ in heads]
    qss = [[] for _ in heads]
    for c in range(nchunk):
        rs = slice(c * GDN_CHUNK, (c + 1) * GDN_CHUNK)
        last = (c + 1) * GDN_CHUNK - 1
        ws = [_dot(jnp.concatenate([w_all[hd][rs], qdec[hd][rs]], axis=0), st[hd]) for hd in heads]
        for hd in heads:
            v_new = u_all[hd][rs] - ws[hd][:GDN_CHUNK]
            qss[hd].append(ws[hd][GDN_CHUNK:])
            vnews[hd].append(v_new)
            gl = gcb[hd][last:last + 1, :]
            kdec = k[hd][rs] * jnp.exp(gl - gcb[hd][rs])
            st[hd] = st[hd] * jnp.exp(gl) + lax.dot_general(
                kdec.astype(BF16), v_new.astype(BF16), (((0,), (0,)), ((), ())), preferred_element_type=F32)
    for hd in heads:
        state[hd] = st[hd]
    for hd in heads:
        o = jnp.concatenate(qss[hd], axis=0) + _dot(qk[hd], jnp.concatenate(vnews[hd], axis=0))
        o = o * lax.rsqrt(jnp.mean(o * o, axis=-1, keepdims=True) + EPS) * nw_ref[...]
        o = o * _silu(z_ref[0, :, lanes[hd]])
        o_ref[0, :, lanes[hd]] = o.astype(o_ref.dtype)


def _gdn(q, k, v, z, gc, gr, nw):
    bsz, seq, w = q.shape
    r = GDN_ROWS
    blk = pl.BlockSpec((1, r, w), lambda b, i: (b, i, 0))
    return pl.pallas_call(
        _gdn_kernel,
        out_shape=jax.ShapeDtypeStruct((bsz, seq, w), BF16),
        grid=(bsz, seq // r),
        in_specs=[blk, blk, blk, blk,
                  pl.BlockSpec((1, r, LANES), lambda b, i: (b, i, 0)),
                  pl.BlockSpec((1, SUBLANES, r), lambda b, i: (b, 0, i)),
                  pl.BlockSpec((1, GDN_HEAD_DIM), lambda b, i: (0, 0))],
        out_specs=blk,
        scratch_shapes=[pltpu.VMEM((GDN_HEADS, GDN_HEAD_DIM, GDN_HEAD_DIM), F32)],
        compiler_params=pltpu.CompilerParams(dimension_semantics=("parallel", "arbitrary"),
                                             vmem_limit_bytes=VMEM_LIMIT),
        name="gdn",
    )(q, k, v, z, gc, gr, nw)


def _rms(x, w):
    return x * lax.rsqrt(jnp.mean(x * x, axis=-1, keepdims=True) + EPS) * w


def _out_ffn_kernel(x_ref, oa_ref, ob_ref, mod_ref, wo_ref, nfw_ref, wfi_ref, wfo_ref, nlw_ref, o_ref, *,
                    final_norm):
    x = x_ref[0]
    mix = (jnp.dot(oa_ref[0], wo_ref[0:CONV_WIDTH, :], preferred_element_type=F32)
           + jnp.dot(ob_ref[0], wo_ref[CONV_WIDTH:, :], preferred_element_type=F32))
    x = x + mod_ref[0, 2:3, :] * mix
    h = _rms(x, nfw_ref[...]) * (1.0 + mod_ref[0, 4:5, :]) + mod_ref[0, 3:4, :]
    hb = h.astype(BF16)
    gate = jnp.dot(hb, wfi_ref[:, :D_FF], preferred_element_type=F32)
    up = jnp.dot(hb, wfi_ref[:, D_FF:], preferred_element_type=F32)
    act = (_silu(gate) * up).astype(BF16)
    ffn = jnp.dot(act, wfo_ref[...], preferred_element_type=F32)
    x = x + mod_ref[0, 5:6, :] * ffn
    o_ref[0] = _rms(x, nlw_ref[...]) if final_norm else x


def _out_ffn(x, oa, ob, mod, wo, nfw, wfi, wfo, nlw, final_norm):
    bsz, seq, d = x.shape
    tm = TM_OUT
    once = lambda shp: pl.BlockSpec(shp, lambda b, i: (0,) * len(shp), pipeline_mode=pl.Buffered(1))
    return pl.pallas_call(
        functools.partial(_out_ffn_kernel, final_norm=final_norm),
        out_shape=jax.ShapeDtypeStruct((bsz, seq, d), F32),
        grid=(bsz, seq // tm),
        in_specs=[pl.BlockSpec((1, tm, d), lambda b, i: (b, i, 0)),
                  pl.BlockSpec((1, tm, CONV_WIDTH), lambda b, i: (b, i, 0)),
                  pl.BlockSpec((1, tm, GDN_WIDTH), lambda b, i: (b, i, 0)),
                  pl.BlockSpec((1, N_MOD, d), lambda b, i: (b, 0, 0)),
                  once(wo.shape), once(nfw.shape), once(wfi.shape), once(wfo.shape), once(nlw.shape)],
        out_specs=pl.BlockSpec((1, tm, d), lambda b, i: (b, i, 0)),
        compiler_params=pltpu.CompilerParams(dimension_semantics=("parallel", "parallel"),
                                             vmem_limit_bytes=VMEM_LIMIT),
        name="out_ffn",
    )(x, oa, ob, mod, wo, nfw, wfi, wfo, nlw)


def _block_diag_const(n, blk, val):
    idx = np.arange(n) // blk
    return jnp.asarray((idx[:, None] == idx[None, :]).astype(np.float32) * val)


def _chunk_tri_const(n, blk):
    i = np.arange(n)
    m = ((i[:, None] // blk) == (i[None, :] // blk)) & (i[:, None] >= i[None, :])
    return jnp.asarray(m.astype(np.float32))


def _pad_lanes(a, n):
    return jnp.pad(a, [(0, 0)] * (a.ndim - 1) + [(0, n - a.shape[-1])])


def kernel(x, c, w_ada, b_ada, norm_mix_w, w_in, conv_w, conv_b, conv_gn_w, conv_gn_b, gdn_conv_w, gdn_a_log, gdn_dt_bias, gdn_norm_w, w_out, norm_ffn_w, w_ffn_in, w_ffn_out, norm_final_w):
    depth = w_ada.shape[0]
    bsz = x.shape[0]
    gmat = _block_diag_const(MXU_DIM, CONV_WIDTH // CONV_GROUPS, 1.0 / (CONV_WIDTH // CONV_GROUPS)).astype(BF16)
    ones_m = _block_diag_const(MXU_DIM, GDN_HEAD_DIM, 1.0).astype(BF16)
    tri = _chunk_tri_const(TM_IN, GDN_CHUNK)
    for layer in range(depth):
        mod = _adaln(c, w_ada[layer], b_ada[layer]).reshape(bsz, N_MOD, D_MODEL)
        wl = w_in[layer]
        o1 = 2 * CONV_WIDTH
        o2 = o1 + 3 * GDN_WIDTH
        o3 = o2 + GDN_WIDTH
        w_ag = wl[:, :o1].astype(BF16)
        w_qkv = wl[:, o1:o2].astype(BF16)
        w_z = wl[:, o2:o3].astype(BF16)
        w_ba = jnp.concatenate([_pad_lanes(wl[:, o3:o3 + GDN_HEADS], LANES),
                                _pad_lanes(wl[:, o3 + GDN_HEADS:], LANES)], axis=-1).astype(BF16)
        oa, qn, kn, vv, z, gc, gr = _mix_in(
            x, mod, norm_mix_w[layer][None, :], w_ag, w_qkv, w_z, w_ba,
            _pad_lanes(conv_w[layer].T, CONV_HALO).T, conv_b[layer][None, :],
            conv_gn_w[layer][None, :], conv_gn_b[layer][None, :], gmat,
            gdn_conv_w[layer], ones_m,
            _pad_lanes(gdn_a_log[layer][None, :], LANES), _pad_lanes(gdn_dt_bias[layer][None, :], LANES), tri)
        ob = _gdn(qn, kn, vv, z, gc, gr, gdn_norm_w[layer][None, :])
        x = _out_ffn(x, oa, ob, mod, w_out[layer].astype(BF16), norm_ffn_w[layer][None, :],
                     w_ffn_in[layer].astype(BF16), w_ffn_out[layer].astype(BF16), norm_final_w[None, :],
                     final_norm=layer == depth - 1)
    return x
```

```python
import functools
import itertools

import jax
import jax.numpy as jnp
import numpy as np
from jax import lax
from jax.experimental import pallas as pl
from jax.experimental.pallas import tpu as pltpu

F32 = jnp.float32
BF16 = jnp.bfloat16

D_MODEL = 1024
CONV_WIDTH = 512
CONV_GROUPS = 8
CONV_KERNEL = 31
GDN_WIDTH = 512
GDN_HEAD_DIM = 128
GDN_HEADS = 4
GDN_SHORT_CONV = 4
GDN_CHUNK = 64
D_FF = 2816
N_MOD = 6
EPS = 1e-6

LANES = 128
SUBLANES = 8
MXU_DIM = 256

CONV_HALO = 32
SC_HALO = SUBLANES
TM = 256
FF_CHUNK = MXU_DIM
VMEM_LIMIT = 58 * 1024 * 1024


def _silu(v):
    return v * jax.nn.sigmoid(v)


def _dot(a, b):
    return jnp.dot(a.astype(BF16), b.astype(BF16), preferred_element_type=F32)


def _rms(x, w):
    return x * lax.rsqrt(jnp.mean(x * x, axis=-1, keepdims=True) + EPS) * w


def _adaln_kernel(c_ref, w_ref, b_ref, o_ref):
    c = c_ref[...]
    o_ref[...] = jnp.dot(_silu(c), w_ref[...], preferred_element_type=F32,
                         precision=lax.Precision.HIGHEST) + b_ref[...]


def _adaln(c, w, b):
    bsz, d = c.shape
    n = w.shape[1]
    tn = 1024
    return pl.pallas_call(
        _adaln_kernel,
        out_shape=jax.ShapeDtypeStruct((bsz, n), F32),
        grid=(n // tn,),
        in_specs=[pl.BlockSpec((bsz, d), lambda j: (0, 0)),
                  pl.BlockSpec((d, tn), lambda j: (0, j)),
                  pl.BlockSpec((1, tn), lambda j: (0, j))],
        out_specs=pl.BlockSpec((bsz, tn), lambda j: (0, j)),
        compiler_params=pltpu.CompilerParams(dimension_semantics=("arbitrary",),
                                             vmem_limit_bytes=VMEM_LIMIT),
        name="adaln",
    )(c, w, b.reshape(1, n))


def _stage_in(x_ref, mod_ref, nw_ref, w_ag_ref, w_qkv_ref, w_z_ref, w_ba_ref,
              cw_ref, cb_ref, gnw_ref, gnb_ref, gmat_ref, scw_ref, ones_ref, alog_ref, dtb_ref, tri_ref,
              hoa, hq, hk, hv, hz, hgc, hgr, ubuf, qbuf):
    tm = x_ref.shape[1]
    h = _rms(x_ref[0], nw_ref[...]) * (1.0 + mod_ref[0, 1:2, :]) + mod_ref[0, 0:1, :]
    hb = h.astype(BF16)

    ag = jnp.dot(hb, w_ag_ref[...], preferred_element_type=F32)
    ubuf[CONV_HALO:CONV_HALO + tm, :] = ag[:, :CONV_WIDTH] * jax.nn.sigmoid(ag[:, CONV_WIDTH:])
    yield
    qbuf[SC_HALO:SC_HALO + tm, :] = jnp.dot(hb, w_qkv_ref[...], preferred_element_type=F32)
    yield
    base = CONV_HALO - (CONV_KERNEL - 1)
    for hp in range(CONV_WIDTH // MXU_DIM):
        half = slice(hp * MXU_DIM, (hp + 1) * MXU_DIM)
        ys = []
        for j in range(hp * MXU_DIM // LANES, (hp + 1) * MXU_DIM // LANES):
            ls = slice(j * LANES, (j + 1) * LANES)
            full = ubuf[:, ls]
            rows = full.shape[0]
            acc = jnp.zeros((tm, LANES), F32) + cb_ref[:, ls]
            for b in range(SUBLANES):
                win = pltpu.roll(full, (rows - base - b) % rows, 0)
                for a in range(-(-(CONV_KERNEL - b) // SUBLANES)):
                    t = SUBLANES * a + b
                    acc = acc + cw_ref[t:t + 1, ls] * win[a * SUBLANES:a * SUBLANES + tm]
            ys.append(acc)
            yield
        y = jnp.concatenate(ys, axis=-1)
        mu = _dot(y, gmat_ref[...])
        d = y - mu
        var = _dot(d * d, gmat_ref[...])
        un = d * lax.rsqrt(var + EPS) * gnw_ref[:, half] + gnb_ref[:, half]
        hoa[:, half] = _silu(un).astype(hoa.dtype)
        yield
    ubuf[0:CONV_HALO, :] = ubuf[tm:tm + CONV_HALO, :]

    sbase = SC_HALO - (GDN_SHORT_CONV - 1)
    outs = (hq, hk, hv)
    for part in range(3):
        for hp in range(GDN_WIDTH // MXU_DIM):
            ls = slice(part * GDN_WIDTH + hp * MXU_DIM, part * GDN_WIDTH + (hp + 1) * MXU_DIM)
            acc = scw_ref[0:1, ls] * qbuf[sbase:sbase + tm, ls]
            for t in range(1, GDN_SHORT_CONV):
                acc = acc + scw_ref[t:t + 1, ls] * qbuf[sbase + t:sbase + t + tm, ls]
            a = _silu(acc)
            if part < 2:
                ss = _dot(a * a, ones_ref[...])
                a = a * lax.rsqrt(ss + EPS)
            outs[part][:, hp * MXU_DIM:(hp + 1) * MXU_DIM] = a
            yield
    qbuf[0:SC_HALO, :] = qbuf[tm:tm + SC_HALO, :]

    hz[...] = jnp.dot(hb, w_z_ref[...], preferred_element_type=F32)

    ba = jnp.dot(hb, w_ba_ref[...], preferred_element_type=F32)
    beta = jax.nn.sigmoid(ba[:, :LANES])
    al = ba[:, LANES:] + dtb_ref[...]
    sp = jnp.maximum(al, 0.0) + jnp.log1p(jnp.exp(-jnp.abs(al)))
    g = -jnp.exp(alog_ref[...]) * sp
    gcum = jnp.dot(tri_ref[...], g, preferred_element_type=F32, precision=lax.Precision.HIGHEST)
    lane = lax.broadcasted_iota(jnp.int32, (tm, LANES), 1)
    packed = jnp.where(lane < GDN_HEADS, gcum, pltpu.roll(beta, GDN_HEADS, 1))
    hgc[...] = packed
    hgr[...] = packed.T[0:SUBLANES, :]
    yield


def _gdn_heads(hq, hk, hv, hz, hgc, hgr, nw_ref, state):
    r = hq.shape[0]
    nchunk = r // GDN_CHUNK
    row = lax.broadcasted_iota(jnp.int32, (r, r), 0)
    col = lax.broadcasted_iota(jnp.int32, (r, r), 1)
    log_c = GDN_CHUNK.bit_length() - 1
    same_chunk = (row >> log_c) == (col >> log_c)
    causal = same_chunk & (row >= col)
    strict = same_chunk & (row > col)
    eye = (row == col).astype(F32)

    gc_all = hgc[...]
    gr_all = hgr[...]
    scale = GDN_HEAD_DIM ** -0.5

    heads = range(GDN_HEADS)
    lanes = [slice(hd * GDN_HEAD_DIM, (hd + 1) * GDN_HEAD_DIM) for hd in heads]
    q = [hq[:, ls] * scale for ls in lanes]
    k = [hk[:, ls] for ls in lanes]
    v = [hv[:, ls] for ls in lanes]
    gcb = [jnp.broadcast_to(gc_all[:, hd:hd + 1], (r, GDN_HEAD_DIM)) for hd in heads]
    bb = [jnp.broadcast_to(gc_all[:, GDN_HEADS + hd:GDN_HEADS + hd + 1], (r, GDN_HEAD_DIM)) for hd in heads]
    decay = []
    for hd in heads:
        gci = jnp.concatenate([gcb[hd]] * (r // GDN_HEAD_DIM), axis=-1)
        gcj = jnp.broadcast_to(gr_all[hd:hd + 1, :], (r, r))
        decay.append(jnp.where(causal, jnp.exp(jnp.where(causal, gci - gcj, 0.0)), 0.0))

    kb = [k[hd] * bb[hd] for hd in heads]
    kk = [lax.dot_general(jnp.concatenate([kb[hd], q[hd]], axis=0).astype(BF16), k[hd].astype(BF16),
                          (((1,), (1,)), ((), ())), preferred_element_type=F32) for hd in heads]
    a_mat = [jnp.where(strict, kk[hd][:r] * decay[hd], 0.0) for hd in heads]
    qk = [kk[hd][r:] * decay[hd] for hd in heads]

    pair = ((row ^ col) >> 1) == 0
    xinv = [eye - jnp.where(pair, a_mat[hd], 0.0) for hd in heads]
    yield
    s = 2
    while s < GDN_CHUNK:
        ls2 = s.bit_length()
        m = (((row ^ col) >> ls2) == 0) & ((row & s) != 0) & ((col & s) == 0)
        xb = [xinv[hd].astype(BF16) for hd in heads]
        ax = [_dot(jnp.where(m, a_mat[hd], 0.0), xb[hd]) for hd in heads]
        xinv = [xinv[hd] - _dot(xb[hd], ax[hd]) for hd in heads]
        s *= 2
        yield

    egc = [jnp.exp(gcb[hd]) for hd in heads]
    uw = [_dot(xinv[hd], jnp.concatenate([v[hd] * bb[hd], kb[hd] * egc[hd]], axis=-1)) for hd in heads]
    u_all = [uw[hd][:, :GDN_HEAD_DIM] for hd in heads]
    w_all = [uw[hd][:, GDN_HEAD_DIM:] for hd in heads]
    qdec = [q[hd] * egc[hd] for hd in heads]
    yield

    st = [state[hd] for hd in heads]
    vnews = [[] for _ in heads]
    qss = [[] for _ in heads]
    for c in range(nchunk):
        rs = slice(c * GDN_CHUNK, (c + 1) * GDN_CHUNK)
        last = (c + 1) * GDN_CHUNK - 1
        ws = [_dot(jnp.concatenate([w_all[hd][rs], qdec[hd][rs]], axis=0), st[hd]) for hd in heads]
        for hd in heads:
            v_new = u_all[hd][rs] - ws[hd][:GDN_CHUNK]
            qss[hd].append(ws[hd][GDN_CHUNK:])
            vnews[hd].append(v_new)
            gl = gcb[hd][last:last + 1, :]
            kdec = k[hd][rs] * jnp.exp(gl - gcb[hd][rs])
            st[hd] = st[hd] * jnp.exp(gl) + lax.dot_general(
                kdec.astype(BF16), v_new.astype(BF16), (((0,), (0,)), ((), ())), preferred_element_type=F32)
        yield
    for hd in heads:
        state[hd] = st[hd]
    outs = []
    for hd in heads:
        o = jnp.concatenate(qss[hd], axis=0) + _dot(qk[hd], jnp.concatenate(vnews[hd], axis=0))
        o = o * lax.rsqrt(jnp.mean(o * o, axis=-1, keepdims=True) + EPS) * nw_ref[...]
        outs.append((o * _silu(hz[:, lanes[hd]])).astype(BF16))
    return outs


def _stage_out(x_ref, mod_ref, gnw_ref, wo_ref, nfw_ref, wfi_ref, wfo_ref, nlw_ref,
               hoa, hq, hk, hv, hz, hgc, hgr, state, o_ref, *, final_norm):
    mix = jnp.dot(hoa[...], wo_ref[0:CONV_WIDTH, :], preferred_element_type=F32)
    ob = yield from _gdn_heads(hq, hk, hv, hz, hgc, hgr, gnw_ref, state)
    mix = mix + jnp.dot(jnp.concatenate(ob, axis=-1), wo_ref[CONV_WIDTH:, :], preferred_element_type=F32)
    x = x_ref[0] + mod_ref[0, 2:3, :] * mix
    hb = (_rms(x, nfw_ref[...]) * (1.0 + mod_ref[0, 4:5, :]) + mod_ref[0, 3:4, :]).astype(BF16)
    yield
    ffn = None
    for j in range(D_FF // FF_CHUNK):
        cs = slice(j * FF_CHUNK, (j + 1) * FF_CHUNK)
        gate = jnp.dot(hb, wfi_ref[:, cs], preferred_element_type=F32)
        up = jnp.dot(hb, wfi_ref[:, D_FF + j * FF_CHUNK:D_FF + (j + 1) * FF_CHUNK], preferred_element_type=F32)
        part = jnp.dot((_silu(gate) * up).astype(BF16), wfo_ref[cs, :], preferred_element_type=F32)
        ffn = part if ffn is None else ffn + part
        yield
    x = x + mod_ref[0, 5:6, :] * ffn
    o_ref[0] = _rms(x, nlw_ref[...]) if final_norm else x
    yield


N_IN_ARGS = 15
N_OUT_ARGS = 6

PIECE_ORDER = "oiooiooooo" + "ioo" + "ioo" + "oi" * 11 + "o"


def _layer_kernel(*refs, tiles_per_seq, final_norm):
    x_in, mod_in, x_out, mod_out = refs[0:4]
    in_args = refs[4:4 + N_IN_ARGS]
    out_args = refs[4 + N_IN_ARGS:4 + N_IN_ARGS + N_OUT_ARGS]
    o_ref = refs[4 + N_IN_ARGS + N_OUT_ARGS]
    hoa, hq, hk, hv, hz, hgc, hgr, ubuf, qbuf, state = refs[5 + N_IN_ARGS + N_OUT_ARGS:]
    hand = (hoa, hq, hk, hv, hz, hgc, hgr)
    t = pl.program_id(0)

    @pl.when(t == 0)
    def _():
        for ref in hand:
            ref[...] = jnp.zeros(ref.shape, ref.dtype)

    @pl.when(t % tiles_per_seq == 0)
    def _():
        ubuf[0:CONV_HALO, :] = jnp.zeros((CONV_HALO, CONV_WIDTH), F32)
        qbuf[0:SC_HALO, :] = jnp.zeros((SC_HALO, 3 * GDN_WIDTH), F32)

    @pl.when(t % tiles_per_seq == 1 % tiles_per_seq)
    def _():
        state[...] = jnp.zeros(state.shape, F32)

    gens = {"o": _stage_out(x_out, mod_out, *out_args, *hand, state, o_ref, final_norm=final_norm),
            "i": _stage_in(x_in, mod_in, *in_args, *hand, ubuf, qbuf)}
    for which in PIECE_ORDER:
        next(gens[which])
    done = object()
    assert all(next(g, done) is done for g in gens.values())


def _layer(x, mod, in_ws, out_ws, final_norm):
    bsz, seq, d = x.shape
    tm = TM
    n = seq // tm
    last = bsz * n - 1
    steps = bsz * n + 1

    def tile_in(t):
        tt = jnp.minimum(t, last)
        return tt // n, tt % n

    def tile_out(t):
        tt = jnp.maximum(t - 1, 0)
        return tt // n, tt % n

    once = lambda a: pl.BlockSpec(a.shape, lambda t: (0,) * a.ndim, pipeline_mode=pl.Buffered(1))
    x_spec = lambda tile: pl.BlockSpec((1, tm, d), lambda t: (*tile(t), 0))
    mod_spec = lambda tile: pl.BlockSpec((1, N_MOD, d), lambda t: (tile(t)[0], 0, 0))
    tri = _chunk_tri_const(tm, GDN_CHUNK)
    gmat = _block_diag_const(MXU_DIM, CONV_WIDTH // CONV_GROUPS, 1.0 / (CONV_WIDTH // CONV_GROUPS)).astype(BF16)
    ones_m = _block_diag_const(MXU_DIM, GDN_HEAD_DIM, 1.0).astype(BF16)
    (nw, w_ag, w_qkv, w_z, w_ba, cw, cb, gnw, gnb, scw, alog, dtb) = in_ws
    in_full = (nw, w_ag, w_qkv, w_z, w_ba, cw, cb, gnw, gnb, gmat, scw, ones_m, alog, dtb, tri)
    assert len(in_full) == N_IN_ARGS and len(out_ws) == N_OUT_ARGS
    return pl.pallas_call(
        functools.partial(_layer_kernel, tiles_per_seq=n, final_norm=final_norm),
        out_shape=jax.ShapeDtypeStruct((bsz, seq, d), F32),
        grid=(steps,),
        in_specs=[x_spec(tile_in), mod_spec(tile_in), x_spec(tile_out), mod_spec(tile_out)]
                 + [once(a) for a in in_full] + [once(a) for a in out_ws],
        out_specs=x_spec(tile_out),
        scratch_shapes=[pltpu.VMEM((tm, CONV_WIDTH), BF16),
                        pltpu.VMEM((tm, GDN_WIDTH), F32), pltpu.VMEM((tm, GDN_WIDTH), F32),
                        pltpu.VMEM((tm, GDN_WIDTH), F32), pltpu.VMEM((tm, GDN_WIDTH), F32),
                        pltpu.VMEM((tm, LANES), F32), pltpu.VMEM((SUBLANES, tm), F32),
                        pltpu.VMEM((CONV_HALO + tm, CONV_WIDTH), F32),
                        pltpu.VMEM((SC_HALO + tm, 3 * GDN_WIDTH), F32),
                        pltpu.VMEM((GDN_HEADS, GDN_HEAD_DIM, GDN_HEAD_DIM), F32)],
        compiler_params=pltpu.CompilerParams(dimension_semantics=("arbitrary",),
                                             vmem_limit_bytes=VMEM_LIMIT),
        name="layer",
    )(x, mod, x, mod, *in_full, *out_ws)


def _block_diag_const(n, blk, val):
    idx = np.arange(n) // blk
    return jnp.asarray((idx[:, None] == idx[None, :]).astype(np.float32) * val)


def _chunk_tri_const(n, blk):
    i = np.arange(n)
    m = ((i[:, None] // blk) == (i[None, :] // blk)) & (i[:, None] >= i[None, :])
    return jnp.asarray(m.astype(np.float32))


def _pad_lanes(a, n):
    return jnp.pad(a, [(0, 0)] * (a.ndim - 1) + [(0, n - a.shape[-1])])


def kernel(x, c, w_ada, b_ada, norm_mix_w, w_in, conv_w, conv_b, conv_gn_w, conv_gn_b, gdn_conv_w, gdn_a_log, gdn_dt_bias, gdn_norm_w, w_out, norm_ffn_w, w_ffn_in, w_ffn_out, norm_final_w):
    depth = w_ada.shape[0]
    bsz = x.shape[0]
    for layer in range(depth):
        mod = _adaln(c, w_ada[layer], b_ada[layer]).reshape(bsz, N_MOD, D_MODEL)
        wl = w_in[layer]
        o1 = 2 * CONV_WIDTH
        o2 = o1 + 3 * GDN_WIDTH
        o3 = o2 + GDN_WIDTH
        w_ba = jnp.concatenate([_pad_lanes(wl[:, o3:o3 + GDN_HEADS], LANES),
                                _pad_lanes(wl[:, o3 + GDN_HEADS:], LANES)], axis=-1).astype(BF16)
        in_ws = (norm_mix_w[layer][None, :], wl[:, :o1].astype(BF16), wl[:, o1:o2].astype(BF16),
                 wl[:, o2:o3].astype(BF16), w_ba,
                 _pad_lanes(conv_w[layer].T, CONV_HALO).T, conv_b[layer][None, :],
                 conv_gn_w[layer][None, :], conv_gn_b[layer][None, :], gdn_conv_w[layer],
                 _pad_lanes(gdn_a_log[layer][None, :], LANES), _pad_lanes(gdn_dt_bias[layer][None, :], LANES))
        out_ws = (gdn_norm_w[layer][None, :], w_out[layer].astype(BF16), norm_ffn_w[layer][None, :],
                  w_ffn_in[layer].astype(BF16), w_ffn_out[layer].astype(BF16), norm_final_w[None, :])
        x = _layer(x, mod, in_ws, out_ws, final_norm=layer == depth - 1)
    return x
```

```python
import functools
import itertools

import jax
import jax.numpy as jnp
import numpy as np
from jax import lax
from jax.experimental import pallas as pl
from jax.experimental.pallas import tpu as pltpu

F32 = jnp.float32
BF16 = jnp.bfloat16

D_MODEL = 1024
CONV_WIDTH = 512
CONV_GROUPS = 8
CONV_KERNEL = 31
GDN_WIDTH = 512
GDN_HEAD_DIM = 128
GDN_HEADS = 4
GDN_SHORT_CONV = 4
GDN_CHUNK = 64
GDN_GROUP = 128
D_FF = 2816
N_MOD = 6
EPS = 1e-6

LANES = 128
SUBLANES = 8
MXU_DIM = 256

CONV_HALO = 32
SC_HALO = SUBLANES
TM = 256
FF_CHUNK = MXU_DIM
VMEM_LIMIT = 58 * 1024 * 1024


def _silu(v):
    return v * jax.nn.sigmoid(v)


def _dot(a, b):
    return jnp.dot(a.astype(BF16), b.astype(BF16), preferred_element_type=F32)


def _rms(x, w):
    return x * lax.rsqrt(jnp.mean(x * x, axis=-1, keepdims=True) + EPS) * w


def _adaln_kernel(c_ref, w_ref, b_ref, o_ref):
    c = c_ref[...]
    o_ref[...] = jnp.dot(_silu(c), w_ref[...], preferred_element_type=F32,
                         precision=lax.Precision.HIGHEST) + b_ref[...]


def _adaln(c, w, b):
    bsz, d = c.shape
    n = w.shape[1]
    tn = 1024
    return pl.pallas_call(
        _adaln_kernel,
        out_shape=jax.ShapeDtypeStruct((bsz, n), F32),
        grid=(n // tn,),
        in_specs=[pl.BlockSpec((bsz, d), lambda j: (0, 0)),
                  pl.BlockSpec((d, tn), lambda j: (0, j)),
                  pl.BlockSpec((1, tn), lambda j: (0, j))],
        out_specs=pl.BlockSpec((bsz, tn), lambda j: (0, j)),
        compiler_params=pltpu.CompilerParams(dimension_semantics=("arbitrary",),
                                             vmem_limit_bytes=VMEM_LIMIT),
        name="adaln",
    )(c, w, b.reshape(1, n))


def _stage_in(x_ref, mod_ref, nw_ref, w_ag_ref, w_qkv_ref, w_z_ref, w_ba_ref,
              cw_ref, cb_ref, gnw_ref, gnb_ref, gmat_ref, scw_ref, ones_ref, alog_ref, dtb_ref, tri_ref,
              hoa, hq, hk, hv, hz, hgc, hgr, ubuf, qbuf):
    tm = x_ref.shape[1]
    h = _rms(x_ref[0], nw_ref[...]) * (1.0 + mod_ref[0, 1:2, :]) + mod_ref[0, 0:1, :]
    hb = h.astype(BF16)

    ag = jnp.dot(hb, w_ag_ref[...], preferred_element_type=F32)
    ubuf[CONV_HALO:CONV_HALO + tm, :] = ag[:, :CONV_WIDTH] * jax.nn.sigmoid(ag[:, CONV_WIDTH:])
    yield
    qbuf[SC_HALO:SC_HALO + tm, :] = jnp.dot(hb, w_qkv_ref[...], preferred_element_type=F32)
    yield
    base = CONV_HALO - (CONV_KERNEL - 1)
    for hp in range(CONV_WIDTH // MXU_DIM):
        half = slice(hp * MXU_DIM, (hp + 1) * MXU_DIM)
        ys = []
        for j in range(hp * MXU_DIM // LANES, (hp + 1) * MXU_DIM // LANES):
            ls = slice(j * LANES, (j + 1) * LANES)
            full = ubuf[:, ls]
            rows = full.shape[0]
            acc = jnp.zeros((tm, LANES), F32) + cb_ref[:, ls]
            for b in range(SUBLANES):
                win = pltpu.roll(full, (rows - base - b) % rows, 0)
                for a in range(-(-(CONV_KERNEL - b) // SUBLANES)):
                    t = SUBLANES * a + b
                    acc = acc + cw_ref[t:t + 1, ls] * win[a * SUBLANES:a * SUBLANES + tm]
            ys.append(acc)
            yield
        y = jnp.concatenate(ys, axis=-1)
        mu = _dot(y, gmat_ref[...])
        d = y - mu
        var = _dot(d * d, gmat_ref[...])
        un = d * lax.rsqrt(var + EPS) * gnw_ref[:, half] + gnb_ref[:, half]
        hoa[:, half] = _silu(un).astype(hoa.dtype)
        yield
    ubuf[0:CONV_HALO, :] = ubuf[tm:tm + CONV_HALO, :]

    sbase = SC_HALO - (GDN_SHORT_CONV - 1)
    outs = (hq, hk, hv)
    for part in range(3):
        for hp in range(GDN_WIDTH // MXU_DIM):
            ls = slice(part * GDN_WIDTH + hp * MXU_DIM, part * GDN_WIDTH + (hp + 1) * MXU_DIM)
            acc = scw_ref[0:1, ls] * qbuf[sbase:sbase + tm, ls]
            for t in range(1, GDN_SHORT_CONV):
                acc = acc + scw_ref[t:t + 1, ls] * qbuf[sbase + t:sbase + t + tm, ls]
            a = _silu(acc)
            if part < 2:
                ss = _dot(a * a, ones_ref[...])
                a = a * lax.rsqrt(ss + EPS)
            outs[part][:, hp * MXU_DIM:(hp + 1) * MXU_DIM] = a
            yield
    qbuf[0:SC_HALO, :] = qbuf[tm:tm + SC_HALO, :]

    hz[...] = jnp.dot(hb, w_z_ref[...], preferred_element_type=F32)

    ba = jnp.dot(hb, w_ba_ref[...], preferred_element_type=F32)
    beta = jax.nn.sigmoid(ba[:, :LANES])
    al = ba[:, LANES:] + dtb_ref[...]
    sp = jnp.maximum(al, 0.0) + jnp.log1p(jnp.exp(-jnp.abs(al)))
    g = -jnp.exp(alog_ref[...]) * sp
    gcum = jnp.zeros_like(g)
    rest = g
    for _ in range(3):
        term = rest.astype(BF16)
        gcum = jnp.dot(tri_ref[...], term, preferred_element_type=F32) + gcum
        rest = rest - term.astype(F32)
    lane = lax.broadcasted_iota(jnp.int32, (tm, LANES), 1)
    packed = jnp.where(lane < GDN_HEADS, gcum, pltpu.roll(beta, GDN_HEADS, 1))
    hgc[...] = packed
    hgr[...] = packed.T[0:SUBLANES, :]
    yield


def _gdn_heads(hq, hk, hv, hz, hgc, hgr, nw_ref, state):
    rows = hq.shape[0]
    r = GDN_GROUP
    assert r == GDN_HEAD_DIM and rows % r == 0 and r % GDN_CHUNK == 0
    nchunk = rows // GDN_CHUNK
    row = lax.broadcasted_iota(jnp.int32, (r, r), 0)
    col = lax.broadcasted_iota(jnp.int32, (r, r), 1)
    log_c = GDN_CHUNK.bit_length() - 1
    same_chunk = (row >> log_c) == (col >> log_c)
    causal = same_chunk & (row >= col)
    strict = same_chunk & (row > col)
    eye = (row == col).astype(F32)

    gc_all = hgc[...]
    gr_all = hgr[...]
    scale = GDN_HEAD_DIM ** -0.5

    heads = range(GDN_HEADS)
    lanes = [slice(hd * GDN_HEAD_DIM, (hd + 1) * GDN_HEAD_DIM) for hd in heads]
    units = [(slice(g * r, (g + 1) * r), hd) for g in range(rows // r) for hd in heads]
    un = range(len(units))
    q = [hq[rs, lanes[hd]] * scale for rs, hd in units]
    k = [hk[rs, lanes[hd]] for rs, hd in units]
    v = [hv[rs, lanes[hd]] for rs, hd in units]
    gcb = [jnp.broadcast_to(gc_all[rs, hd:hd + 1], (r, GDN_HEAD_DIM)) for rs, hd in units]
    bb = [jnp.broadcast_to(gc_all[rs, GDN_HEADS + hd:GDN_HEADS + hd + 1], (r, GDN_HEAD_DIM)) for rs, hd in units]
    decay = []
    for u, (rs, hd) in enumerate(units):
        gcj = jnp.broadcast_to(gr_all[hd:hd + 1, rs], (r, r))
        decay.append(jnp.where(causal, jnp.exp(jnp.where(causal, gcb[u] - gcj, 0.0)), 0.0))

    kb = [k[u] * bb[u] for u in un]
    kk = [lax.dot_general(jnp.concatenate([kb[u], q[u]], axis=0).astype(BF16), k[u].astype(BF16),
                          (((1,), (1,)), ((), ())), preferred_element_type=F32) for u in un]
    a_mat = [jnp.where(strict, kk[u][:r] * decay[u], 0.0) for u in un]
    qk = [kk[u][r:] * decay[u] for u in un]

    pair = ((row ^ col) >> 1) == 0
    xinv = [eye - jnp.where(pair, a_mat[u], 0.0) for u in un]
    yield
    s = 2
    while s < GDN_CHUNK:
        ls2 = s.bit_length()
        m = (((row ^ col) >> ls2) == 0) & ((row & s) != 0) & ((col & s) == 0)
        xb = [xinv[u].astype(BF16) for u in un]
        ax = [_dot(jnp.where(m, a_mat[u], 0.0), xb[u]) for u in un]
        xinv = [xinv[u] - _dot(xb[u], ax[u]) for u in un]
        s *= 2
        yield

    egc = [jnp.exp(gcb[u]) for u in un]
    uw = [_dot(xinv[u], jnp.concatenate([v[u] * bb[u], kb[u] * egc[u]], axis=-1)) for u in un]
    qdec = [q[u] * egc[u] for u in un]
    yield

    st = [state[hd] for hd in heads]
    vnews = [[] for _ in un]
    qss = [[] for _ in un]
    for c in range(nchunk):
        g, cs = divmod(c * GDN_CHUNK, r)
        rs = slice(cs, cs + GDN_CHUNK)
        last = cs + GDN_CHUNK - 1
        us = [g * GDN_HEADS + hd for hd in heads]
        ws = [_dot(jnp.concatenate([uw[u][rs, GDN_HEAD_DIM:], qdec[u][rs]], axis=0), st[hd])
              for hd, u in zip(heads, us)]
        for hd, u in zip(heads, us):
            v_new = uw[u][rs, :GDN_HEAD_DIM] - ws[hd][:GDN_CHUNK]
            qss[u].append(ws[hd][GDN_CHUNK:])
            vnews[u].append(v_new)
            gl = gcb[u][last:last + 1, :]
            kdec = k[u][rs] * jnp.exp(gl - gcb[u][rs])
            st[hd] = st[hd] * jnp.exp(gl) + lax.dot_general(
                kdec.astype(BF16), v_new.astype(BF16), (((0,), (0,)), ((), ())), preferred_element_type=F32)
        yield
    for hd in heads:
        state[hd] = st[hd]
    outs = []
    for hd in heads:
        o = jnp.concatenate([jnp.concatenate(qss[u], axis=0) + _dot(qk[u], jnp.concatenate(vnews[u], axis=0))
                             for u in range(hd, len(units), GDN_HEADS)], axis=0)
        o = o * lax.rsqrt(jnp.mean(o * o, axis=-1, keepdims=True) + EPS) * nw_ref[...]
        outs.append((o * _silu(hz[:, lanes[hd]])).astype(BF16))
    return outs


def _stage_out(x_ref, mod_ref, gnw_ref, wo_ref, nfw_ref, wfi_ref, wfo_ref, nlw_ref,
               hoa, hq, hk, hv, hz, hgc, hgr, state, o_ref, *, final_norm):
    mix = jnp.dot(hoa[...], wo_ref[0:CONV_WIDTH, :], preferred_element_type=F32)
    ob = yield from _gdn_heads(hq, hk, hv, hz, hgc, hgr, gnw_ref, state)
    mix = jnp.dot(jnp.concatenate(ob, axis=-1), wo_ref[CONV_WIDTH:, :], preferred_element_type=F32) + mix
    x = x_ref[0] + mod_ref[0, 2:3, :] * mix
    hb = (_rms(x, nfw_ref[...]) * (1.0 + mod_ref[0, 4:5, :]) + mod_ref[0, 3:4, :]).astype(BF16)
    yield
    ffn = None
    for j in range(D_FF // FF_CHUNK):
        cs = slice(j * FF_CHUNK, (j + 1) * FF_CHUNK)
        gate = jnp.dot(hb, wfi_ref[:, cs], preferred_element_type=F32)
        up = jnp.dot(hb, wfi_ref[:, D_FF + j * FF_CHUNK:D_FF + (j + 1) * FF_CHUNK], preferred_element_type=F32)
        part = jnp.dot((_silu(gate) * up).astype(BF16), wfo_ref[cs, :], preferred_element_type=F32)
        ffn = part if ffn is None else part + ffn
        yield
    x = x + mod_ref[0, 5:6, :] * ffn
    o_ref[0] = _rms(x, nlw_ref[...]) if final_norm else x
    yield


N_IN_ARGS = 15
N_OUT_ARGS = 6

PIECE_ORDER = "oiooiooooo" + "ioo" + "ioo" + "oi" * 11 + "o"


def _layer_kernel(*refs, tiles_per_seq, final_norm):
    x_in, mod_in, x_out, mod_out = refs[0:4]
    in_args = refs[4:4 + N_IN_ARGS]
    out_args = refs[4 + N_IN_ARGS:4 + N_IN_ARGS + N_OUT_ARGS]
    o_ref = refs[4 + N_IN_ARGS + N_OUT_ARGS]
    hoa, hq, hk, hv, hz, hgc, hgr, ubuf, qbuf, state = refs[5 + N_IN_ARGS + N_OUT_ARGS:]
    hand = (hoa, hq, hk, hv, hz, hgc, hgr)
    t = pl.program_id(0)

    @pl.when(t == 0)
    def _():
        for ref in hand:
            ref[...] = jnp.zeros(ref.shape, ref.dtype)

    @pl.when(t % tiles_per_seq == 0)
    def _():
        ubuf[0:CONV_HALO, :] = jnp.zeros((CONV_HALO, CONV_WIDTH), F32)
        qbuf[0:SC_HALO, :] = jnp.zeros((SC_HALO, 3 * GDN_WIDTH), F32)

    @pl.when(t % tiles_per_seq == 1 % tiles_per_seq)
    def _():
        state[...] = jnp.zeros(state.shape, F32)

    gens = {"o": _stage_out(x_out, mod_out, *out_args, *hand, state, o_ref, final_norm=final_norm),
            "i": _stage_in(x_in, mod_in, *in_args, *hand, ubuf, qbuf)}
    for which in PIECE_ORDER:
        next(gens[which])
    done = object()
    assert all(next(g, done) is done for g in gens.values())


def _layer(x, mod, in_ws, out_ws, final_norm):
    bsz, seq, d = x.shape
    tm = TM
    n = seq // tm
    last = bsz * n - 1
    steps = bsz * n + 1

    def tile_in(t):
        tt = jnp.minimum(t, last)
        return tt // n, tt % n

    def tile_out(t):
        tt = jnp.maximum(t - 1, 0)
        return tt // n, tt % n

    once = lambda a: pl.BlockSpec(a.shape, lambda t: (0,) * a.ndim, pipeline_mode=pl.Buffered(1))
    x_spec = lambda tile: pl.BlockSpec((1, tm, d), lambda t: (*tile(t), 0))
    mod_spec = lambda tile: pl.BlockSpec((1, N_MOD, d), lambda t: (tile(t)[0], 0, 0))
    tri = _chunk_tri_const(tm, GDN_CHUNK).astype(BF16)
    gmat = _block_diag_const(MXU_DIM, CONV_WIDTH // CONV_GROUPS, 1.0 / (CONV_WIDTH // CONV_GROUPS)).astype(BF16)
    ones_m = _block_diag_const(MXU_DIM, GDN_HEAD_DIM, 1.0).astype(BF16)
    (nw, w_ag, w_qkv, w_z, w_ba, cw, cb, gnw, gnb, scw, alog, dtb) = in_ws
    in_full = (nw, w_ag, w_qkv, w_z, w_ba, cw, cb, gnw, gnb, gmat, scw, ones_m, alog, dtb, tri)
    assert len(in_full) == N_IN_ARGS and len(out_ws) == N_OUT_ARGS
    return pl.pallas_call(
        functools.partial(_layer_kernel, tiles_per_seq=n, final_norm=final_norm),
        out_shape=jax.ShapeDtypeStruct((bsz, seq, d), F32),
        grid=(steps,),
        in_specs=[x_spec(tile_in), mod_spec(tile_in), x_spec(tile_out), mod_spec(tile_out)]
                 + [once(a) for a in in_full] + [once(a) for a in out_ws],
        out_specs=x_spec(tile_out),
        scratch_shapes=[pltpu.VMEM((tm, CONV_WIDTH), BF16),
                        pltpu.VMEM((tm, GDN_WIDTH), F32), pltpu.VMEM((tm, GDN_WIDTH), F32),
                        pltpu.VMEM((tm, GDN_WIDTH), F32), pltpu.VMEM((tm, GDN_WIDTH), F32),
                        pltpu.VMEM((tm, LANES), F32), pltpu.VMEM((SUBLANES, tm), F32),
                        pltpu.VMEM((CONV_HALO + tm, CONV_WIDTH), F32),
                        pltpu.VMEM((SC_HALO + tm, 3 * GDN_WIDTH), F32),
                        pltpu.VMEM((GDN_HEADS, GDN_HEAD_DIM, GDN_HEAD_DIM), F32)],
        compiler_params=pltpu.CompilerParams(dimension_semantics=("arbitrary",),
                                             vmem_limit_bytes=VMEM_LIMIT),
        name="layer",
    )(x, mod, x, mod, *in_full, *out_ws)


def _block_diag_const(n, blk, val):
    idx = np.arange(n) // blk
    return jnp.asarray((idx[:, None] == idx[None, :]).astype(np.float32) * val)


def _chunk_tri_const(n, blk):
    i = np.arange(n)
    m = ((i[:, None] // blk) == (i[None, :] // blk)) & (i[:, None] >= i[None, :])
    return jnp.asarray(m.astype(np.float32))


def _pad_lanes(a, n):
    return jnp.pad(a, [(0, 0)] * (a.ndim - 1) + [(0, n - a.shape[-1])])


def kernel(x, c, w_ada, b_ada, norm_mix_w, w_in, conv_w, conv_b, conv_gn_w, conv_gn_b, gdn_conv_w, gdn_a_log, gdn_dt_bias, gdn_norm_w, w_out, norm_ffn_w, w_ffn_in, w_ffn_out, norm_final_w):
    depth = w_ada.shape[0]
    bsz = x.shape[0]
    for layer in range(depth):
        mod = _adaln(c, w_ada[layer], b_ada[layer]).reshape(bsz, N_MOD, D_MODEL)
        wl = w_in[layer]
        o1 = 2 * CONV_WIDTH
        o2 = o1 + 3 * GDN_WIDTH
        o3 = o2 + GDN_WIDTH
        w_ba = jnp.concatenate([_pad_lanes(wl[:, o3:o3 + GDN_HEADS], LANES),
                                _pad_lanes(wl[:, o3 + GDN_HEADS:], LANES)], axis=-1).astype(BF16)
        in_ws = (norm_mix_w[layer][None, :], wl[:, :o1].astype(BF16), wl[:, o1:o2].astype(BF16),
                 wl[:, o2:o3].astype(BF16), w_ba,
                 _pad_lanes(conv_w[layer].T, CONV_HALO).T, conv_b[layer][None, :],
                 conv_gn_w[layer][None, :], conv_gn_b[layer][None, :], gdn_conv_w[layer],
                 _pad_lanes(gdn_a_log[layer][None, :], LANES), _pad_lanes(gdn_dt_bias[layer][None, :], LANES))
        out_ws = (gdn_norm_w[layer][None, :], w_out[layer].astype(BF16), norm_ffn_w[layer][None, :],
                  w_ffn_in[layer].astype(BF16), w_ffn_out[layer].astype(BF16), norm_final_w[None, :])
        x = _layer(x, mod, in_ws, out_ws, final_norm=layer == depth - 1)
    return x
```

```python
import functools
import itertools

import jax
import jax.numpy as jnp
import numpy as np
from jax import lax
from jax.experimental import pallas as pl
from jax.experimental.pallas import tpu as pltpu

F32 = jnp.float32
BF16 = jnp.bfloat16

D_MODEL = 1024
CONV_WIDTH = 512
CONV_GROUPS = 8
CONV_KERNEL = 31
GDN_WIDTH = 512
GDN_HEAD_DIM = 128
GDN_HEADS = 4
GDN_SHORT_CONV = 4
GDN_CHUNK = 64
GDN_GROUP = 128
D_FF = 2816
N_MOD = 6
EPS = 1e-6

LANES = 128
SUBLANES = 8
MXU_DIM = 256

CONV_HALO = 32
SC_HALO = SUBLANES
TM = 256
FF_CHUNK = D_FF
VMEM_LIMIT = 58 * 1024 * 1024


def _silu(v):
    return v * jax.nn.sigmoid(v)


def _dot(a, b):
    return jnp.dot(a.astype(BF16), b.astype(BF16), preferred_element_type=F32)


def _rms(x, w):
    return x * lax.rsqrt(jnp.mean(x * x, axis=-1, keepdims=True) + EPS) * w


def _adaln_kernel(c_ref, w_ref, b_ref, o_ref):
    c = c_ref[...]
    o_ref[...] = jnp.dot(_silu(c), w_ref[...], preferred_element_type=F32,
                         precision=lax.Precision.HIGHEST) + b_ref[...]


def _adaln(c, w, b):
    bsz, d = c.shape
    n = w.shape[1]
    tn = 1024
    return pl.pallas_call(
        _adaln_kernel,
        out_shape=jax.ShapeDtypeStruct((bsz, n), F32),
        grid=(n // tn,),
        in_specs=[pl.BlockSpec((bsz, d), lambda j: (0, 0)),
                  pl.BlockSpec((d, tn), lambda j: (0, j)),
                  pl.BlockSpec((1, tn), lambda j: (0, j))],
        out_specs=pl.BlockSpec((bsz, tn), lambda j: (0, j)),
        compiler_params=pltpu.CompilerParams(dimension_semantics=("arbitrary",),
                                             vmem_limit_bytes=VMEM_LIMIT),
        name="adaln",
    )(c, w, b.reshape(1, n))


def _stage_in(x_ref, mod_ref, nw_ref, w_ag_ref, w_qkv_ref, w_z_ref, w_ba_ref,
              cw_ref, cb_ref, gnw_ref, gnb_ref, gmat_ref, scw_ref, ones_ref, alog_ref, dtb_ref, tri_ref,
              hoa, hq, hk, hv, hz, hgc, hgr, ubuf, qbuf):
    tm = x_ref.shape[1]
    h = _rms(x_ref[0], nw_ref[...]) * (1.0 + mod_ref[0, 1:2, :]) + mod_ref[0, 0:1, :]
    hb = h.astype(BF16)

    ag = jnp.dot(hb, w_ag_ref[...], preferred_element_type=F32)
    ubuf[CONV_HALO:CONV_HALO + tm, :] = ag[:, :CONV_WIDTH] * jax.nn.sigmoid(ag[:, CONV_WIDTH:])
    yield
    qbuf[SC_HALO:SC_HALO + tm, :] = jnp.dot(hb, w_qkv_ref[...], preferred_element_type=F32)
    yield

    z = jnp.dot(hb, w_z_ref[...], preferred_element_type=F32)
    ba = jnp.dot(hb, w_ba_ref[...], preferred_element_type=F32)
    beta = jax.nn.sigmoid(ba[:, :LANES])
    al = ba[:, LANES:] + dtb_ref[...]
    sp = jnp.maximum(al, 0.0) + jnp.log1p(jnp.exp(-jnp.abs(al)))
    g = -jnp.exp(alog_ref[...]) * sp
    gcum = jnp.zeros_like(g)
    rest = g
    for _ in range(3):
        term = rest.astype(BF16)
        gcum = jnp.dot(tri_ref[...], term, preferred_element_type=F32) + gcum
        rest = rest - term.astype(F32)
    lane = lax.broadcasted_iota(jnp.int32, (tm, LANES), 1)
    packed = jnp.where(lane < GDN_HEADS, gcum, pltpu.roll(beta, GDN_HEADS, 1))
    packed_t = packed.T[0:SUBLANES, :]
    yield
    base = CONV_HALO - (CONV_KERNEL - 1)
    for hp in range(CONV_WIDTH // MXU_DIM):
        half = slice(hp * MXU_DIM, (hp + 1) * MXU_DIM)
        ys = []
        for j in range(hp * MXU_DIM // LANES, (hp + 1) * MXU_DIM // LANES):
            ls = slice(j * LANES, (j + 1) * LANES)
            full = ubuf[:, ls]
            rows = full.shape[0]
            acc = jnp.zeros((tm, LANES), F32) + cb_ref[:, ls]
            for b in range(SUBLANES):
                win = pltpu.roll(full, (rows - base - b) % rows, 0)
                for a in range(-(-(CONV_KERNEL - b) // SUBLANES)):
                    t = SUBLANES * a + b
                    acc = acc + cw_ref[t:t + 1, ls] * win[a * SUBLANES:a * SUBLANES + tm]
            ys.append(acc)
            yield
        y = jnp.concatenate(ys, axis=-1)
        mu = _dot(y, gmat_ref[...])
        d = y - mu
        var = _dot(d * d, gmat_ref[...])
        un = d * lax.rsqrt(var + EPS) * gnw_ref[:, half] + gnb_ref[:, half]
        hoa[:, half] = _silu(un).astype(hoa.dtype)
        yield
    ubuf[0:CONV_HALO, :] = ubuf[tm:tm + CONV_HALO, :]

    sbase = SC_HALO - (GDN_SHORT_CONV - 1)
    outs = (hq, hk, hv)
    for part in range(3):
        for hp in range(GDN_WIDTH // MXU_DIM):
            ls = slice(part * GDN_WIDTH + hp * MXU_DIM, part * GDN_WIDTH + (hp + 1) * MXU_DIM)
            acc = scw_ref[0:1, ls] * qbuf[sbase:sbase + tm, ls]
            for t in range(1, GDN_SHORT_CONV):
                acc = acc + scw_ref[t:t + 1, ls] * qbuf[sbase + t:sbase + t + tm, ls]
            a = _silu(acc)
            if part < 2:
                ss = _dot(a * a, ones_ref[...])
                a = a * lax.rsqrt(ss + EPS)
            outs[part][:, hp * MXU_DIM:(hp + 1) * MXU_DIM] = a
            yield
    qbuf[0:SC_HALO, :] = qbuf[tm:tm + SC_HALO, :]

    hz[...] = z
    hgc[...] = packed
    hgr[...] = packed_t
    yield


def _gdn_heads(hq, hk, hv, hz, hgc, hgr, nw_ref, state):
    rows = hq.shape[0]
    r = GDN_GROUP
    assert r == GDN_HEAD_DIM and rows % r == 0 and r % GDN_CHUNK == 0
    nchunk = rows // GDN_CHUNK
    row = lax.broadcasted_iota(jnp.int32, (r, r), 0)
    col = lax.broadcasted_iota(jnp.int32, (r, r), 1)
    log_c = GDN_CHUNK.bit_length() - 1
    same_chunk = (row >> log_c) == (col >> log_c)
    causal = same_chunk & (row >= col)
    strict = same_chunk & (row > col)
    eye = (row == col).astype(F32)

    gc_all = hgc[...]
    gr_all = hgr[...]
    scale = GDN_HEAD_DIM ** -0.5

    heads = range(GDN_HEADS)
    lanes = [slice(hd * GDN_HEAD_DIM, (hd + 1) * GDN_HEAD_DIM) for hd in heads]
    units = [(slice(g * r, (g + 1) * r), hd) for g in range(rows // r) for hd in heads]
    un = range(len(units))
    q = [hq[rs, lanes[hd]] * scale for rs, hd in units]
    k = [hk[rs, lanes[hd]] for rs, hd in units]
    v = [hv[rs, lanes[hd]] for rs, hd in units]
    gcb = [jnp.broadcast_to(gc_all[rs, hd:hd + 1], (r, GDN_HEAD_DIM)) for rs, hd in units]
    bb = [jnp.broadcast_to(gc_all[rs, GDN_HEADS + hd:GDN_HEADS + hd + 1], (r, GDN_HEAD_DIM)) for rs, hd in units]
    decay = []
    for u, (rs, hd) in enumerate(units):
        gcj = jnp.broadcast_to(gr_all[hd:hd + 1, rs], (r, r))
        decay.append(jnp.where(causal, jnp.exp(jnp.where(causal, gcb[u] - gcj, 0.0)), 0.0))

    kb = [k[u] * bb[u] for u in un]
    kk = [lax.dot_general(jnp.concatenate([kb[u], q[u]], axis=0).astype(BF16), k[u].astype(BF16),
                          (((1,), (1,)), ((), ())), preferred_element_type=F32) for u in un]
    a_mat = [jnp.where(strict, kk[u][:r] * decay[u], 0.0) for u in un]
    qk = [kk[u][r:] * decay[u] for u in un]

    pair = ((row ^ col) >> 1) == 0
    xinv = [eye - jnp.where(pair, a_mat[u], 0.0) for u in un]
    yield
    s = 2
    while s < GDN_CHUNK:
        ls2 = s.bit_length()
        m = (((row ^ col) >> ls2) == 0) & ((row & s) != 0) & ((col & s) == 0)
        xb = [xinv[u].astype(BF16) for u in un]
        ax = [_dot(jnp.where(m, a_mat[u], 0.0), xb[u]) for u in un]
        xinv = [xinv[u] - _dot(xb[u], ax[u]) for u in un]
        s *= 2
        yield

    egc = [jnp.exp(gcb[u]) for u in un]
    uw = [_dot(xinv[u], jnp.concatenate([v[u] * bb[u], kb[u] * egc[u]], axis=-1)) for u in un]
    qdec = [q[u] * egc[u] for u in un]
    yield

    st = [state[hd] for hd in heads]
    vnews = [[] for _ in un]
    qss = [[] for _ in un]
    for c in range(nchunk):
        g, cs = divmod(c * GDN_CHUNK, r)
        rs = slice(cs, cs + GDN_CHUNK)
        last = cs + GDN_CHUNK - 1
        us = [g * GDN_HEADS + hd for hd in heads]
        ws = [_dot(jnp.concatenate([uw[u][rs, GDN_HEAD_DIM:], qdec[u][rs]], axis=0), st[hd])
              for hd, u in zip(heads, us)]
        for hd, u in zip(heads, us):
            v_new = uw[u][rs, :GDN_HEAD_DIM] - ws[hd][:GDN_CHUNK]
            qss[u].append(ws[hd][GDN_CHUNK:])
            vnews[u].append(v_new)
            gl = gcb[u][last:last + 1, :]
            kdec = k[u][rs] * jnp.exp(gl - gcb[u][rs])
            st[hd] = st[hd] * jnp.exp(gl) + lax.dot_general(
                kdec.astype(BF16), v_new.astype(BF16), (((0,), (0,)), ((), ())), preferred_element_type=F32)
        yield
    for hd in heads:
        state[hd] = st[hd]
    outs = []
    for hd in heads:
        o = jnp.concatenate([jnp.concatenate(qss[u], axis=0) + _dot(qk[u], jnp.concatenate(vnews[u], axis=0))
                             for u in range(hd, len(units), GDN_HEADS)], axis=0)
        o = o * lax.rsqrt(jnp.mean(o * o, axis=-1, keepdims=True) + EPS) * nw_ref[...]
        outs.append((o * _silu(hz[:, lanes[hd]])).astype(BF16))
    return outs


def _stage_out(x_ref, mod_ref, gnw_ref, wo_ref, nfw_ref, wfi_ref, wfo_ref, nlw_ref,
               hoa, hq, hk, hv, hz, hgc, hgr, state, o_ref, *, final_norm):
    mix = jnp.dot(hoa[...], wo_ref[0:CONV_WIDTH, :], preferred_element_type=F32)
    ob = yield from _gdn_heads(hq, hk, hv, hz, hgc, hgr, gnw_ref, state)
    mix = jnp.dot(jnp.concatenate(ob, axis=-1), wo_ref[CONV_WIDTH:, :], preferred_element_type=F32) + mix
    x = x_ref[0] + mod_ref[0, 2:3, :] * mix
    hb = (_rms(x, nfw_ref[...]) * (1.0 + mod_ref[0, 4:5, :]) + mod_ref[0, 3:4, :]).astype(BF16)
    yield
    ffn = None
    for j in range(D_FF // FF_CHUNK):
        cs = slice(j * FF_CHUNK, (j + 1) * FF_CHUNK)
        gate = jnp.dot(hb, wfi_ref[:, cs], preferred_element_type=F32)
        up = jnp.dot(hb, wfi_ref[:, D_FF + j * FF_CHUNK:D_FF + (j + 1) * FF_CHUNK], preferred_element_type=F32)
        part = jnp.dot((_silu(gate) * up).astype(BF16), wfo_ref[cs, :], preferred_element_type=F32)
        ffn = part if ffn is None else part + ffn
        yield
    x = x + mod_ref[0, 5:6, :] * ffn
    o_ref[0] = _rms(x, nlw_ref[...]) if final_norm else x
    yield


N_IN_ARGS = 15
N_OUT_ARGS = 6

PIECE_ORDER = "oiooioiooooiooioo" + "iiii" + "o" * (D_FF // FF_CHUNK) + "iiii" + "o" + "iii"


def _layer_kernel(*refs, tiles_per_seq, final_norm):
    x_in, mod_in, x_out, mod_out = refs[0:4]
    in_args = refs[4:4 + N_IN_ARGS]
    out_args = refs[4 + N_IN_ARGS:4 + N_IN_ARGS + N_OUT_ARGS]
    o_ref = refs[4 + N_IN_ARGS + N_OUT_ARGS]
    hoa, hq, hk, hv, hz, hgc, hgr, ubuf, qbuf, state = refs[5 + N_IN_ARGS + N_OUT_ARGS:]
    hand = (hoa, hq, hk, hv, hz, hgc, hgr)
    t = pl.program_id(0)

    @pl.when(t == 0)
    def _():
        for ref in hand:
            ref[...] = jnp.zeros(ref.shape, ref.dtype)

    @pl.when(t % tiles_per_seq == 0)
    def _():
        ubuf[0:CONV_HALO, :] = jnp.zeros((CONV_HALO, CONV_WIDTH), F32)
        qbuf[0:SC_HALO, :] = jnp.zeros((SC_HALO, 3 * GDN_WIDTH), F32)

    @pl.when(t % tiles_per_seq == 1 % tiles_per_seq)
    def _():
        state[...] = jnp.zeros(state.shape, F32)

    gens = {"o": _stage_out(x_out, mod_out, *out_args, *hand, state, o_ref, final_norm=final_norm),
            "i": _stage_in(x_in, mod_in, *in_args, *hand, ubuf, qbuf)}
    for which in PIECE_ORDER:
        next(gens[which])
    done = object()
    assert all(next(g, done) is done for g in gens.values())


def _layer(x, mod, in_ws, out_ws, final_norm):
    bsz, seq, d = x.shape
    tm = TM
    n = seq // tm
    last = bsz * n - 1
    steps = bsz * n + 1

    def tile_in(t):
        tt = jnp.minimum(t, last)
        return tt // n, tt % n

    def tile_out(t):
        tt = jnp.maximum(t - 1, 0)
        return tt // n, tt % n

    once = lambda a: pl.BlockSpec(a.shape, lambda t: (0,) * a.ndim, pipeline_mode=pl.Buffered(1))
    x_spec = lambda tile: pl.BlockSpec((1, tm, d), lambda t: (*tile(t), 0))
    mod_spec = lambda tile: pl.BlockSpec((1, N_MOD, d), lambda t: (tile(t)[0], 0, 0))
    tri = _chunk_tri_const(tm, GDN_CHUNK).astype(BF16)
    gmat = _block_diag_const(MXU_DIM, CONV_WIDTH // CONV_GROUPS, 1.0 / (CONV_WIDTH // CONV_GROUPS)).astype(BF16)
    ones_m = _block_diag_const(MXU_DIM, GDN_HEAD_DIM, 1.0).astype(BF16)
    (nw, w_ag, w_qkv, w_z, w_ba, cw, cb, gnw, gnb, scw, alog, dtb) = in_ws
    in_full = (nw, w_ag, w_qkv, w_z, w_ba, cw, cb, gnw, gnb, gmat, scw, ones_m, alog, dtb, tri)
    assert len(in_full) == N_IN_ARGS and len(out_ws) == N_OUT_ARGS
    return pl.pallas_call(
        functools.partial(_layer_kernel, tiles_per_seq=n, final_norm=final_norm),
        out_shape=jax.ShapeDtypeStruct((bsz, seq, d), F32),
        grid=(steps,),
        in_specs=[x_spec(tile_in), mod_spec(tile_in), x_spec(tile_out), mod_spec(tile_out)]
                 + [once(a) for a in in_full] + [once(a) for a in out_ws],
        out_specs=x_spec(tile_out),
        scratch_shapes=[pltpu.VMEM((tm, CONV_WIDTH), BF16),
                        pltpu.VMEM((tm, GDN_WIDTH), F32), pltpu.VMEM((tm, GDN_WIDTH), F32),
                        pltpu.VMEM((tm, GDN_WIDTH), F32), pltpu.VMEM((tm, GDN_WIDTH), F32),
                        pltpu.VMEM((tm, LANES), F32), pltpu.VMEM((SUBLANES, tm), F32),
                        pltpu.VMEM((CONV_HALO + tm, CONV_WIDTH), F32),
                        pltpu.VMEM((SC_HALO + tm, 3 * GDN_WIDTH), F32),
                        pltpu.VMEM((GDN_HEADS, GDN_HEAD_DIM, GDN_HEAD_DIM), F32)],
        compiler_params=pltpu.CompilerParams(dimension_semantics=("arbitrary",),
                                             vmem_limit_bytes=VMEM_LIMIT),
        name="layer",
    )(x, mod, x, mod, *in_full, *out_ws)


def _block_diag_const(n, blk, val):
    idx = np.arange(n) // blk
    return jnp.asarray((idx[:, None] == idx[None, :]).astype(np.float32) * val)


def _chunk_tri_const(n, blk):
    i = np.arange(n)
    m = ((i[:, None] // blk) == (i[None, :] // blk)) & (i[:, None] >= i[None, :])
    return jnp.asarray(m.astype(np.float32))


def _pad_lanes(a, n):
    return jnp.pad(a, [(0, 0)] * (a.ndim - 1) + [(0, n - a.shape[-1])])


def kernel(x, c, w_ada, b_ada, norm_mix_w, w_in, conv_w, conv_b, conv_gn_w, conv_gn_b, gdn_conv_w, gdn_a_log, gdn_dt_bias, gdn_norm_w, w_out, norm_ffn_w, w_ffn_in, w_ffn_out, norm_final_w):
    depth = w_ada.shape[0]
    bsz = x.shape[0]
    for layer in range(depth):
        mod = _adaln(c, w_ada[layer], b_ada[layer]).reshape(bsz, N_MOD, D_MODEL)
        wl = w_in[layer]
        o1 = 2 * CONV_WIDTH
        o2 = o1 + 3 * GDN_WIDTH
        o3 = o2 + GDN_WIDTH
        w_ba = jnp.concatenate([_pad_lanes(wl[:, o3:o3 + GDN_HEADS], LANES),
                                _pad_lanes(wl[:, o3 + GDN_HEADS:], LANES)], axis=-1).astype(BF16)
        in_ws = (norm_mix_w[layer][None, :], wl[:, :o1].astype(BF16), wl[:, o1:o2].astype(BF16),
                 wl[:, o2:o3].astype(BF16), w_ba,
                 _pad_lanes(conv_w[layer].T, CONV_HALO).T, conv_b[layer][None, :],
                 conv_gn_w[layer][None, :], conv_gn_b[layer][None, :], gdn_conv_w[layer],
                 _pad_lanes(gdn_a_log[layer][None, :], LANES), _pad_lanes(gdn_dt_bias[layer][None, :], LANES))
        out_ws = (gdn_norm_w[layer][None, :], w_out[layer].astype(BF16), norm_ffn_w[layer][None, :],
                  w_ffn_in[layer].astype(BF16), w_ffn_out[layer].astype(BF16), norm_final_w[None, :])
        x = _layer(x, mod, in_ws, out_ws, final_norm=layer == depth - 1)
    return x
```

```python
import functools
import itertools

import jax
import jax.numpy as jnp
import numpy as np
from jax import lax
from jax.experimental import pallas as pl
from jax.experimental.pallas import tpu as pltpu

F32 = jnp.float32
BF16 = jnp.bfloat16

D_MODEL = 1024
CONV_WIDTH = 512
CONV_GROUPS = 8
CONV_KERNEL = 31
GDN_WIDTH = 512
GDN_HEAD_DIM = 128
GDN_HEADS = 4
GDN_SHORT_CONV = 4
GDN_CHUNK = 64
GDN_GROUP = 128
D_FF = 2816
N_MOD = 6
EPS = 1e-6

LANES = 128
SUBLANES = 8
MXU_DIM = 256

CONV_HALO = 32
SC_HALO = SUBLANES
TM = 256
FF_CHUNK = D_FF
VMEM_LIMIT = 58 * 1024 * 1024


def _silu(v):
    h = 0.5 * v
    return h + h * jnp.tanh(h)


def _dot(a, b):
    return jnp.dot(a.astype(BF16), b.astype(BF16), preferred_element_type=F32)


def _rms(x, w):
    return x * lax.rsqrt(jnp.mean(x * x, axis=-1, keepdims=True) + EPS) * w


def _rms_mod(x, w, shift, scale):
    return _rms(x, w * (1.0 + scale)) + shift


def _adaln_kernel(c_ref, w_ref, b_ref, o_ref):
    c = c_ref[...]
    o_ref[...] = jnp.dot(_silu(c), w_ref[...], preferred_element_type=F32,
                         precision=lax.Precision.HIGHEST) + b_ref[...]


def _adaln(c, w, b):
    bsz, d = c.shape
    n = w.shape[1]
    tn = 1024
    return pl.pallas_call(
        _adaln_kernel,
        out_shape=jax.ShapeDtypeStruct((bsz, n), F32),
        grid=(n // tn,),
        in_specs=[pl.BlockSpec((bsz, d), lambda j: (0, 0)),
                  pl.BlockSpec((d, tn), lambda j: (0, j)),
                  pl.BlockSpec((1, tn), lambda j: (0, j))],
        out_specs=pl.BlockSpec((bsz, tn), lambda j: (0, j)),
        compiler_params=pltpu.CompilerParams(dimension_semantics=("arbitrary",),
                                             vmem_limit_bytes=VMEM_LIMIT),
        name="adaln",
    )(c, w, b.reshape(1, n))


def _stage_in(x_ref, mod_ref, nw_ref, w_ag_ref, w_qkv_ref, w_z_ref, w_ba_ref,
              cw_ref, cb_ref, gnw_ref, gnb_ref, gmat_ref, scw_ref, alog_ref, dtb_ref, tri_ref,
              hoa, hq, hk, hv, hz, hgc, hgr, ubuf, qbuf):
    tm = x_ref.shape[1]
    hb = _rms_mod(x_ref[0], nw_ref[...], mod_ref[0, 0:1, :], mod_ref[0, 1:2, :]).astype(BF16)

    ag = jnp.dot(hb, w_ag_ref[...], preferred_element_type=F32)
    ubuf[CONV_HALO:CONV_HALO + tm, :] = ag[:, :CONV_WIDTH] * jax.nn.sigmoid(ag[:, CONV_WIDTH:])
    yield
    qbuf[SC_HALO:SC_HALO + tm, :] = jnp.dot(hb, w_qkv_ref[...], preferred_element_type=F32)
    yield

    z = jnp.dot(hb, w_z_ref[...], preferred_element_type=F32)
    ba = jnp.dot(hb, w_ba_ref[...], preferred_element_type=F32)
    beta = jax.nn.sigmoid(ba)
    al = ba + dtb_ref[...]
    sp = jnp.maximum(al, 0.0) + jnp.log1p(jnp.exp(-jnp.abs(al)))
    g = -jnp.exp(alog_ref[...]) * sp
    gcum = jnp.zeros_like(g)
    rest = g
    for _ in range(3):
        term = rest.astype(BF16)
        gcum = jnp.dot(tri_ref[...], term, preferred_element_type=F32) + gcum
        rest = rest - term.astype(F32)
    lane = lax.broadcasted_iota(jnp.int32, (tm, LANES), 1)
    packed = jnp.where(lane < GDN_HEADS, beta, gcum)
    packed_t = packed.T[0:SUBLANES, :]
    yield
    base = CONV_HALO - (CONV_KERNEL - 1)
    for hp in range(CONV_WIDTH // MXU_DIM):
        half = slice(hp * MXU_DIM, (hp + 1) * MXU_DIM)
        ys = []
        for j in range(hp * MXU_DIM // LANES, (hp + 1) * MXU_DIM // LANES):
            ls = slice(j * LANES, (j + 1) * LANES)
            full = ubuf[:, ls]
            rows = full.shape[0]
            acc = jnp.zeros((tm, LANES), F32) + cb_ref[:, ls]
            for b in range(SUBLANES):
                win = pltpu.roll(full, (rows - base - b) % rows, 0)
                for a in range(-(-(CONV_KERNEL - b) // SUBLANES)):
                    t = SUBLANES * a + b
                    acc = acc + cw_ref[t:t + 1, ls] * win[a * SUBLANES:a * SUBLANES + tm]
            ys.append(acc)
            yield
        y = jnp.concatenate(ys, axis=-1)
        mu = _dot(y, gmat_ref[...])
        d = y - mu
        var = _dot(d * d, gmat_ref[...])
        un = d * lax.rsqrt(var + EPS) * gnw_ref[:, half] + gnb_ref[:, half]
        hoa[:, half] = _silu(un).astype(hoa.dtype)
        yield
    ubuf[0:CONV_HALO, :] = ubuf[tm:tm + CONV_HALO, :]

    sbase = SC_HALO - (GDN_SHORT_CONV - 1)
    outs = (hq, hk, hv)
    for part in range(3):
        for hp in range(GDN_WIDTH // MXU_DIM):
            ls = slice(part * GDN_WIDTH + hp * MXU_DIM, part * GDN_WIDTH + (hp + 1) * MXU_DIM)
            acc = scw_ref[0:1, ls] * qbuf[sbase:sbase + tm, ls]
            for t in range(1, GDN_SHORT_CONV):
                acc = acc + scw_ref[t:t + 1, ls] * qbuf[sbase + t:sbase + t + tm, ls]
            a = _silu(acc)
            if part < 2:
                hs = [a[:, j:j + GDN_HEAD_DIM] for j in range(0, MXU_DIM, GDN_HEAD_DIM)]
                a = jnp.concatenate(
                    [h * lax.rsqrt(jnp.sum(h * h, axis=-1, keepdims=True) + EPS) for h in hs], axis=-1)
            outs[part][:, hp * MXU_DIM:(hp + 1) * MXU_DIM] = a
            yield
    qbuf[0:SC_HALO, :] = qbuf[tm:tm + SC_HALO, :]

    hz[...] = z
    hgc[...] = packed
    hgr[...] = packed_t
    yield


def _gdn_heads(hq, hk, hv, hz, hgc, hgr, nw_ref, state):
    rows = hq.shape[0]
    r = GDN_GROUP
    assert r == GDN_HEAD_DIM and rows % r == 0 and r % GDN_CHUNK == 0
    nchunk = rows // GDN_CHUNK
    row = lax.broadcasted_iota(jnp.int32, (r, r), 0)
    col = lax.broadcasted_iota(jnp.int32, (r, r), 1)
    log_c = GDN_CHUNK.bit_length() - 1
    same_chunk = (row >> log_c) == (col >> log_c)
    causal = same_chunk & (row >= col)
    strict = same_chunk & (row > col)
    eye = (row == col).astype(F32)

    gc_all = hgc[...]
    gr_all = hgr[...]
    scale = GDN_HEAD_DIM ** -0.5

    heads = range(GDN_HEADS)
    lanes = [slice(hd * GDN_HEAD_DIM, (hd + 1) * GDN_HEAD_DIM) for hd in heads]
    units = [(slice(g * r, (g + 1) * r), hd) for g in range(rows // r) for hd in heads]
    un = range(len(units))
    q = [hq[rs, lanes[hd]] * scale for rs, hd in units]
    k = [hk[rs, lanes[hd]] for rs, hd in units]
    v = [hv[rs, lanes[hd]] for rs, hd in units]
    gcb = [jnp.broadcast_to(gc_all[rs, GDN_HEADS + hd:GDN_HEADS + hd + 1], (r, GDN_HEAD_DIM))
           for rs, hd in units]
    bb = [jnp.broadcast_to(gc_all[rs, hd:hd + 1], (r, GDN_HEAD_DIM)) for rs, hd in units]
    decay = []
    for u, (rs, hd) in enumerate(units):
        gcj = jnp.broadcast_to(gr_all[GDN_HEADS + hd:GDN_HEADS + hd + 1, rs], (r, r))
        decay.append(jnp.where(causal, jnp.exp(jnp.where(causal, gcb[u] - gcj, 0.0)), 0.0))

    kb = [k[u] * bb[u] for u in un]
    kk = [lax.dot_general(jnp.concatenate([kb[u], q[u]], axis=0).astype(BF16), k[u].astype(BF16),
                          (((1,), (1,)), ((), ())), preferred_element_type=F32) for u in un]
    a_mat = [jnp.where(strict, kk[u][:r] * decay[u], 0.0) for u in un]
    qk = [kk[u][r:] * decay[u] for u in un]

    pair = ((row ^ col) >> 1) == 0
    xinv = [eye - jnp.where(pair, a_mat[u], 0.0) for u in un]
    yield
    s = 2
    while s < GDN_CHUNK:
        ls2 = s.bit_length()
        m = (((row ^ col) >> ls2) == 0) & ((row & s) != 0) & ((col & s) == 0)
        xb = [xinv[u].astype(BF16) for u in un]
        ax = [_dot(jnp.where(m, a_mat[u], 0.0), xb[u]) for u in un]
        xinv = [xinv[u] - _dot(xb[u], ax[u]) for u in un]
        s *= 2
        yield

    egc = [jnp.exp(gcb[u]) for u in un]
    uw = [_dot(xinv[u], jnp.concatenate([v[u] * bb[u], kb[u] * egc[u]], axis=-1)) for u in un]
    qdec = [q[u] * egc[u] for u in un]
    yield

    st = [state[hd] for hd in heads]
    vnews = [[] for _ in un]
    qss = [[] for _ in un]
    for c in range(nchunk):
        g, cs = divmod(c * GDN_CHUNK, r)
        rs = slice(cs, cs + GDN_CHUNK)
        last = cs + GDN_CHUNK - 1
        us = [g * GDN_HEADS + hd for hd in heads]
        ws = [_dot(jnp.concatenate([uw[u][rs, GDN_HEAD_DIM:], qdec[u][rs]], axis=0), st[hd])
              for hd, u in zip(heads, us)]
        for hd, u in zip(heads, us):
            v_new = uw[u][rs, :GDN_HEAD_DIM] - ws[hd][:GDN_CHUNK]
            qss[u].append(ws[hd][GDN_CHUNK:])
            vnews[u].append(v_new)
            gl = gcb[u][last:last + 1, :]
            kdec = k[u][rs] * jnp.exp(gl - gcb[u][rs])
            st[hd] = st[hd] * jnp.exp(gl) + lax.dot_general(
                kdec.astype(BF16), v_new.astype(BF16), (((0,), (0,)), ((), ())), preferred_element_type=F32)
        yield
    for hd in heads:
        state[hd] = st[hd]
    outs = []
    for hd in heads:
        o = jnp.concatenate([jnp.concatenate(qss[u], axis=0) + _dot(qk[u], jnp.concatenate(vnews[u], axis=0))
                             for u in range(hd, len(units), GDN_HEADS)], axis=0)
        o = o * lax.rsqrt(jnp.mean(o * o, axis=-1, keepdims=True) + EPS) * nw_ref[...]
        outs.append((o * _silu(hz[:, lanes[hd]])).astype(BF16))
    return outs


def _stage_out(x_ref, mod_ref, gnw_ref, wo_ref, nfw_ref, wfi_ref, wfo_ref, nlw_ref,
               hoa, hq, hk, hv, hz, hgc, hgr, state, o_ref, *, final_norm):
    mix = jnp.dot(hoa[...], wo_ref[0:CONV_WIDTH, :], preferred_element_type=F32)
    ob = yield from _gdn_heads(hq, hk, hv, hz, hgc, hgr, gnw_ref, state)
    mix = jnp.dot(jnp.concatenate(ob, axis=-1), wo_ref[CONV_WIDTH:, :], preferred_element_type=F32) + mix
    x = x_ref[0] + mod_ref[0, 2:3, :] * mix
    hb = _rms_mod(x, nfw_ref[...], mod_ref[0, 3:4, :], mod_ref[0, 4:5, :]).astype(BF16)
    yield
    ffn = None
    for j in range(D_FF // FF_CHUNK):
        cs = slice(j * FF_CHUNK, (j + 1) * FF_CHUNK)
        gate = jnp.dot(hb, wfi_ref[:, cs], preferred_element_type=F32)
        up = jnp.dot(hb, wfi_ref[:, D_FF + j * FF_CHUNK:D_FF + (j + 1) * FF_CHUNK], preferred_element_type=F32)
        part = jnp.dot((_silu(gate) * up).astype(BF16), wfo_ref[cs, :], preferred_element_type=F32)
        ffn = part if ffn is None else part + ffn
        yield
    x = x + mod_ref[0, 5:6, :] * ffn
    o_ref[0] = _rms(x, nlw_ref[...]) if final_norm else x
    yield


N_IN_ARGS = 14
N_OUT_ARGS = 6

PIECE_ORDER = "oiooioiooooiooioo" + "iiii" + "o" * (D_FF // FF_CHUNK) + "iiii" + "o" + "iii"


def _layer_kernel(*refs, tiles_per_seq, final_norm):
    x_in, mod_in, x_out, mod_out = refs[0:4]
    in_args = refs[4:4 + N_IN_ARGS]
    out_args = refs[4 + N_IN_ARGS:4 + N_IN_ARGS + N_OUT_ARGS]
    o_ref = refs[4 + N_IN_ARGS + N_OUT_ARGS]
    hoa, hq, hk, hv, hz, hgc, hgr, ubuf, qbuf, state = refs[5 + N_IN_ARGS + N_OUT_ARGS:]
    hand = (hoa, hq, hk, hv, hz, hgc, hgr)
    t = pl.program_id(0)

    @pl.when(t == 0)
    def _():
        for ref in hand:
            ref[...] = jnp.zeros(ref.shape, ref.dtype)

    @pl.when(t % tiles_per_seq == 0)
    def _():
        ubuf[0:CONV_HALO, :] = jnp.zeros((CONV_HALO, CONV_WIDTH), F32)
        qbuf[0:SC_HALO, :] = jnp.zeros((SC_HALO, 3 * GDN_WIDTH), F32)

    @pl.when(t % tiles_per_seq == 1 % tiles_per_seq)
    def _():
        state[...] = jnp.zeros(state.shape, F32)

    gens = {"o": _stage_out(x_out, mod_out, *out_args, *hand, state, o_ref, final_norm=final_norm),
            "i": _stage_in(x_in, mod_in, *in_args, *hand, ubuf, qbuf)}
    for which in PIECE_ORDER:
        next(gens[which])
    done = object()
    assert all(next(g, done) is done for g in gens.values())


def _layer(x, mod, in_ws, out_ws, final_norm):
    bsz, seq, d = x.shape
    tm = TM
    n = seq // tm
    last = bsz * n - 1
    steps = bsz * n + 1

    def tile_in(t):
        tt = jnp.minimum(t, last)
        return tt // n, tt % n

    def tile_out(t):
        tt = jnp.maximum(t - 1, 0)
        return tt // n, tt % n

    once = lambda a: pl.BlockSpec(a.shape, lambda t: (0,) * a.ndim, pipeline_mode=pl.Buffered(1))
    x_spec = lambda tile: pl.BlockSpec((1, tm, d), lambda t: (*tile(t), 0))
    mod_spec = lambda tile: pl.BlockSpec((1, N_MOD, d), lambda t: (tile(t)[0], 0, 0))
    tri = _chunk_tri_const(tm, GDN_CHUNK).astype(BF16)
    gmat = _block_diag_const(MXU_DIM, CONV_WIDTH // CONV_GROUPS, 1.0 / (CONV_WIDTH // CONV_GROUPS)).astype(BF16)
    (nw, w_ag, w_qkv, w_z, w_ba, cw, cb, gnw, gnb, scw, alog, dtb) = in_ws
    in_full = (nw, w_ag, w_qkv, w_z, w_ba, cw, cb, gnw, gnb, gmat, scw, alog, dtb, tri)
    assert len(in_full) == N_IN_ARGS and len(out_ws) == N_OUT_ARGS
    return pl.pallas_call(
        functools.partial(_layer_kernel, tiles_per_seq=n, final_norm=final_norm),
        out_shape=jax.ShapeDtypeStruct((bsz, seq, d), F32),
        grid=(steps,),
        in_specs=[x_spec(tile_in), mod_spec(tile_in), x_spec(tile_out), mod_spec(tile_out)]
                 + [once(a) for a in in_full] + [once(a) for a in out_ws],
        out_specs=x_spec(tile_out),
        scratch_shapes=[pltpu.VMEM((tm, CONV_WIDTH), BF16),
                        pltpu.VMEM((tm, GDN_WIDTH), F32), pltpu.VMEM((tm, GDN_WIDTH), F32),
                        pltpu.VMEM((tm, GDN_WIDTH), F32), pltpu.VMEM((tm, GDN_WIDTH), F32),
                        pltpu.VMEM((tm, LANES), F32), pltpu.VMEM((SUBLANES, tm), F32),
                        pltpu.VMEM((CONV_HALO + tm, CONV_WIDTH), F32),
                        pltpu.VMEM((SC_HALO + tm, 3 * GDN_WIDTH), F32),
                        pltpu.VMEM((GDN_HEADS, GDN_HEAD_DIM, GDN_HEAD_DIM), F32)],
        compiler_params=pltpu.CompilerParams(dimension_semantics=("arbitrary",),
                                             vmem_limit_bytes=VMEM_LIMIT),
        name="layer",
    )(x, mod, x, mod, *in_full, *out_ws)


def _block_diag_const(n, blk, val):
    idx = np.arange(n) // blk
    return jnp.asarray((idx[:, None] == idx[None, :]).astype(np.float32) * val)


def _chunk_tri_const(n, blk):
    i = np.arange(n)
    m = ((i[:, None] // blk) == (i[None, :] // blk)) & (i[:, None] >= i[None, :])
    return jnp.asarray(m.astype(np.float32))


def _pad_lanes(a, n):
    return jnp.pad(a, [(0, 0)] * (a.ndim - 1) + [(0, n - a.shape[-1])])


def kernel(x, c, w_ada, b_ada, norm_mix_w, w_in, conv_w, conv_b, conv_gn_w, conv_gn_b, gdn_conv_w, gdn_a_log, gdn_dt_bias, gdn_norm_w, w_out, norm_ffn_w, w_ffn_in, w_ffn_out, norm_final_w):
    depth = w_ada.shape[0]
    bsz = x.shape[0]
    for layer in range(depth):
        mod = _adaln(c, w_ada[layer], b_ada[layer]).reshape(bsz, N_MOD, D_MODEL)
        wl = w_in[layer]
        o1 = 2 * CONV_WIDTH
        o2 = o1 + 3 * GDN_WIDTH
        o3 = o2 + GDN_WIDTH
        w_ba = _pad_lanes(wl[:, o3:], LANES).astype(BF16)
        head_row = lambda a: _pad_lanes(jnp.concatenate([jnp.zeros_like(a), a])[None, :], LANES)
        in_ws = (norm_mix_w[layer][None, :], wl[:, :o1].astype(BF16), wl[:, o1:o2].astype(BF16),
                 wl[:, o2:o3].astype(BF16), w_ba,
                 _pad_lanes(conv_w[layer].T, CONV_HALO).T, conv_b[layer][None, :],
                 conv_gn_w[layer][None, :], conv_gn_b[layer][None, :], gdn_conv_w[layer],
                 head_row(gdn_a_log[layer]), head_row(gdn_dt_bias[layer]))
        out_ws = (gdn_norm_w[layer][None, :], w_out[layer].astype(BF16), norm_ffn_w[layer][None, :],
                  w_ffn_in[layer].astype(BF16), w_ffn_out[layer].astype(BF16), norm_final_w[None, :])
        x = _layer(x, mod, in_ws, out_ws, final_norm=layer == depth - 1)
    return x
```

```python
import functools
import itertools

import jax
import jax.numpy as jnp
import numpy as np
from jax import lax
from jax.experimental import pallas as pl
from jax.experimental.pallas import tpu as pltpu

F32 = jnp.float32
BF16 = jnp.bfloat16

D_MODEL = 1024
CONV_WIDTH = 512
CONV_GROUPS = 8
CONV_KERNEL = 31
GDN_WIDTH = 512
GDN_HEAD_DIM = 128
GDN_HEADS = 4
GDN_SHORT_CONV = 4
GDN_CHUNK = 64
GDN_GROUP = 128
D_FF = 2816
N_MOD = 6
EPS = 1e-6

LANES = 128
SUBLANES = 8
MXU_DIM = 256

CONV_HALO = 32
SC_HALO = SUBLANES
TM = 256
FF_CHUNK = D_FF
VMEM_LIMIT = 58 * 1024 * 1024


def _silu_of_twice(h):
    return h + h * jnp.tanh(h)


def _silu(v):
    return _silu_of_twice(0.5 * v)


def _dot(a, b):
    return jnp.dot(a.astype(BF16), b.astype(BF16), preferred_element_type=F32)


def _rms(x, w):
    return x * lax.rsqrt(jnp.mean(x * x, axis=-1, keepdims=True) + EPS) * w


def _rms_mod(x, w, shift, scale):
    return _rms(x, w * (1.0 + scale)) + shift


def _adaln_kernel(c_ref, w_ref, b_ref, o_ref):
    c = c_ref[...]
    o_ref[...] = jnp.dot(_silu(c), w_ref[...], preferred_element_type=F32,
                         precision=lax.Precision.HIGHEST) + b_ref[...]


def _adaln(c, w, b):
    bsz, d = c.shape
    n = w.shape[1]
    tn = 1024
    return pl.pallas_call(
        _adaln_kernel,
        out_shape=jax.ShapeDtypeStruct((bsz, n), F32),
        grid=(n // tn,),
        in_specs=[pl.BlockSpec((bsz, d), lambda j: (0, 0)),
                  pl.BlockSpec((d, tn), lambda j: (0, j)),
                  pl.BlockSpec((1, tn), lambda j: (0, j))],
        out_specs=pl.BlockSpec((bsz, tn), lambda j: (0, j)),
        compiler_params=pltpu.CompilerParams(dimension_semantics=("arbitrary",),
                                             vmem_limit_bytes=VMEM_LIMIT),
        name="adaln",
    )(c, w, b.reshape(1, n))


def _stage_in(x_ref, mod_ref, nw_ref, w_ag_ref, w_qkv_ref, w_z_ref, w_ba_ref,
              cw_ref, cb_ref, gnw_ref, gnb_ref, gmat_ref, scw_ref, alog_ref, dtb_ref, tri_ref,
              hoa, hq, hk, hv, hz, hgc, hgr, ubuf, qbuf):
    tm = x_ref.shape[1]
    hb = _rms_mod(x_ref[0], nw_ref[...], mod_ref[0, 0:1, :], mod_ref[0, 1:2, :]).astype(BF16)

    ag = jnp.dot(hb, w_ag_ref[...], preferred_element_type=F32)
    ubuf[CONV_HALO:CONV_HALO + tm, :] = ag[:, :CONV_WIDTH] * jax.nn.sigmoid(ag[:, CONV_WIDTH:])
    yield
    qbuf[SC_HALO:SC_HALO + tm, :] = jnp.dot(hb, w_qkv_ref[...], preferred_element_type=F32)
    yield

    z = jnp.dot(hb, w_z_ref[...], preferred_element_type=F32)
    ba = jnp.dot(hb, w_ba_ref[...], preferred_element_type=F32)
    beta = jax.nn.sigmoid(ba)
    al = ba + dtb_ref[...]
    sp = jnp.maximum(al, 0.0) + jnp.log1p(jnp.exp(-jnp.abs(al)))
    g = -jnp.exp(alog_ref[...]) * sp
    gcum = jnp.zeros_like(g)
    rest = g
    for _ in range(3):
        term = rest.astype(BF16)
        gcum = jnp.dot(tri_ref[...], term, preferred_element_type=F32) + gcum
        rest = rest - term.astype(F32)
    lane = lax.broadcasted_iota(jnp.int32, (tm, LANES), 1)
    packed = jnp.where(lane < GDN_HEADS, beta, gcum)
    packed_t = packed.T[0:SUBLANES, :]
    yield
    base = CONV_HALO - (CONV_KERNEL - 1)
    for hp in range(CONV_WIDTH // MXU_DIM):
        half = slice(hp * MXU_DIM, (hp + 1) * MXU_DIM)
        ys = []
        for j in range(hp * MXU_DIM // LANES, (hp + 1) * MXU_DIM // LANES):
            ls = slice(j * LANES, (j + 1) * LANES)
            full = ubuf[:, ls]
            rows = full.shape[0]
            acc = jnp.zeros((tm, LANES), F32) + cb_ref[:, ls]
            for b in range(SUBLANES):
                win = pltpu.roll(full, (rows - base - b) % rows, 0)
                for a in range(-(-(CONV_KERNEL - b) // SUBLANES)):
                    t = SUBLANES * a + b
                    acc = acc + cw_ref[t:t + 1, ls] * win[a * SUBLANES:a * SUBLANES + tm]
            ys.append(acc)
            yield
        y = jnp.concatenate(ys, axis=-1)
        mu = _dot(y, gmat_ref[...])
        d = y - mu
        var = _dot(d * d, gmat_ref[...])
        un_half = d * lax.rsqrt(var + EPS) * (0.5 * gnw_ref[:, half]) + 0.5 * gnb_ref[:, half]
        hoa[:, half] = _silu_of_twice(un_half).astype(hoa.dtype)
        yield
    ubuf[0:CONV_HALO, :] = ubuf[tm:tm + CONV_HALO, :]

    sbase = SC_HALO - (GDN_SHORT_CONV - 1)
    outs = (hq, hk, hv)
    for part in range(3):
        for hp in range(GDN_WIDTH // MXU_DIM):
            ls = slice(part * GDN_WIDTH + hp * MXU_DIM, part * GDN_WIDTH + (hp + 1) * MXU_DIM)
            acc = (0.5 * scw_ref[0:1, ls]) * qbuf[sbase:sbase + tm, ls]
            for t in range(1, GDN_SHORT_CONV):
                acc = acc + (0.5 * scw_ref[t:t + 1, ls]) * qbuf[sbase + t:sbase + t + tm, ls]
            a = _silu_of_twice(acc)
            if part < 2:
                hs = [a[:, j:j + GDN_HEAD_DIM] for j in range(0, MXU_DIM, GDN_HEAD_DIM)]
                a = jnp.concatenate(
                    [h * lax.rsqrt(jnp.sum(h * h, axis=-1, keepdims=True) + EPS) for h in hs], axis=-1)
            outs[part][:, hp * MXU_DIM:(hp + 1) * MXU_DIM] = a
            yield
    qbuf[0:SC_HALO, :] = qbuf[tm:tm + SC_HALO, :]

    hz[...] = z
    hgc[...] = packed
    hgr[...] = packed_t
    yield


def _gdn_heads(hq, hk, hv, hz, hgc, hgr, nw_ref, state):
    rows = hq.shape[0]
    r = GDN_GROUP
    assert r == GDN_HEAD_DIM and rows % r == 0 and r % GDN_CHUNK == 0
    nchunk = rows // GDN_CHUNK
    row = lax.broadcasted_iota(jnp.int32, (r, r), 0)
    col = lax.broadcasted_iota(jnp.int32, (r, r), 1)
    log_c = GDN_CHUNK.bit_length() - 1
    same_chunk = (row >> log_c) == (col >> log_c)
    causal = same_chunk & (row >= col)
    strict = same_chunk & (row > col)
    eye = (row == col).astype(F32)

    gc_all = hgc[...]
    gr_all = hgr[...]
    scale = GDN_HEAD_DIM ** -0.5

    heads = range(GDN_HEADS)
    lanes = [slice(hd * GDN_HEAD_DIM, (hd + 1) * GDN_HEAD_DIM) for hd in heads]
    units = [(slice(g * r, (g + 1) * r), hd) for g in range(rows // r) for hd in heads]
    un = range(len(units))
    q = [hq[rs, lanes[hd]] * scale for rs, hd in units]
    k = [hk[rs, lanes[hd]] for rs, hd in units]
    v = [hv[rs, lanes[hd]] for rs, hd in units]
    gcb = [jnp.broadcast_to(gc_all[rs, GDN_HEADS + hd:GDN_HEADS + hd + 1], (r, GDN_HEAD_DIM))
           for rs, hd in units]
    bb = [jnp.broadcast_to(gc_all[rs, hd:hd + 1], (r, GDN_HEAD_DIM)) for rs, hd in units]
    decay = []
    for u, (rs, hd) in enumerate(units):
        gcj = jnp.broadcast_to(gr_all[GDN_HEADS + hd:GDN_HEADS + hd + 1, rs], (r, r))
        decay.append(jnp.where(causal, jnp.exp(jnp.where(causal, gcb[u] - gcj, 0.0)), 0.0))

    kb = [k[u] * bb[u] for u in un]
    kk = [lax.dot_general(jnp.concatenate([kb[u], q[u]], axis=0).astype(BF16), k[u].astype(BF16),
                          (((1,), (1,)), ((), ())), preferred_element_type=F32) for u in un]
    a_mat = [jnp.where(strict, kk[u][:r] * decay[u], 0.0) for u in un]
    qk = [kk[u][r:] * decay[u] for u in un]

    pair = ((row ^ col) >> 1) == 0
    xinv = [eye - jnp.where(pair, a_mat[u], 0.0) for u in un]
    yield
    s = 2
    while s < GDN_CHUNK:
        ls2 = s.bit_length()
        m = (((row ^ col) >> ls2) == 0) & ((row & s) != 0) & ((col & s) == 0)
        xb = [xinv[u].astype(BF16) for u in un]
        ax = [_dot(jnp.where(m, a_mat[u], 0.0), xb[u]) for u in un]
        xinv = [xinv[u] - _dot(xb[u], ax[u]) for u in un]
        s *= 2
        yield

    egc = [jnp.exp(gcb[u]) for u in un]
    uw = [_dot(xinv[u], jnp.concatenate([v[u] * bb[u], kb[u] * egc[u]], axis=-1)) for u in un]
    qdec = [q[u] * egc[u] for u in un]
    yield

    terms = []
    for c in range(nchunk):
        g, cs = divmod(c * GDN_CHUNK, r)
        rs = slice(cs, cs + GDN_CHUNK)
        last = cs + GDN_CHUNK - 1
        per_head = []
        for hd in heads:
            u = g * GDN_HEADS + hd
            gl = gcb[u][last:last + 1, :]
            kdec = k[u][rs] * jnp.exp(gl - gcb[u][rs])
            kuw = lax.dot_general(kdec.astype(BF16), uw[u][rs].astype(BF16), (((0,), (0,)), ((), ())),
                                  preferred_element_type=F32)
            per_head.append((u, rs, jnp.exp(gl), kuw[:, :GDN_HEAD_DIM], kuw[:, GDN_HEAD_DIM:]))
        terms.append(per_head)
    yield

    st = [state[hd] for hd in heads]
    vnews = [[] for _ in un]
    qss = [[] for _ in un]
    for per_head in terms:
        prods = [_dot(jnp.concatenate([kw, uw[u][rs, GDN_HEAD_DIM:], qdec[u][rs]], axis=0), st[hd])
                 for hd, (u, rs, _, _, kw) in zip(heads, per_head)]
        for hd, (u, rs, e, ku, _) in zip(heads, per_head):
            vnews[u].append(uw[u][rs, :GDN_HEAD_DIM] - prods[hd][GDN_HEAD_DIM:GDN_HEAD_DIM + GDN_CHUNK])
            qss[u].append(prods[hd][GDN_HEAD_DIM + GDN_CHUNK:])
            st[hd] = st[hd] * e + ku - prods[hd][:GDN_HEAD_DIM]
        yield
    for hd in heads:
        state[hd] = st[hd]
    outs = []
    for hd in heads:
        o = jnp.concatenate([jnp.concatenate(qss[u], axis=0) + _dot(qk[u], jnp.concatenate(vnews[u], axis=0))
                             for u in range(hd, len(units), GDN_HEADS)], axis=0)
        o = o * lax.rsqrt(jnp.mean(o * o, axis=-1, keepdims=True) + EPS) * nw_ref[...]
        outs.append((o * _silu_of_twice(hz[:, lanes[hd]])).astype(BF16))
    return outs


def _stage_out(x_ref, mod_ref, gnw_ref, wo_ref, nfw_ref, wfi_ref, wfo_ref, nlw_ref,
               hoa, hq, hk, hv, hz, hgc, hgr, state, o_ref, *, final_norm):
    mix = jnp.dot(hoa[...], wo_ref[0:CONV_WIDTH, :], preferred_element_type=F32)
    ob = yield from _gdn_heads(hq, hk, hv, hz, hgc, hgr, gnw_ref, state)
    mix = jnp.dot(jnp.concatenate(ob, axis=-1), wo_ref[CONV_WIDTH:, :], preferred_element_type=F32) + mix
    x = x_ref[0] + mod_ref[0, 2:3, :] * mix
    hb = _rms_mod(x, nfw_ref[...], mod_ref[0, 3:4, :], mod_ref[0, 4:5, :]).astype(BF16)
    yield
    ffn = None
    for j in range(D_FF // FF_CHUNK):
        cs = slice(j * FF_CHUNK, (j + 1) * FF_CHUNK)
        gate_half = jnp.dot(hb, wfi_ref[:, cs], preferred_element_type=F32).astype(BF16)
        up = jnp.dot(hb, wfi_ref[:, D_FF + j * FF_CHUNK:D_FF + (j + 1) * FF_CHUNK],
                     preferred_element_type=F32).astype(BF16)
        part = jnp.dot(_silu_of_twice(gate_half) * up, wfo_ref[cs, :], preferred_element_type=F32)
        ffn = part if ffn is None else part + ffn
        yield
    x = x + mod_ref[0, 5:6, :] * ffn
    o_ref[0] = _rms(x, nlw_ref[...]) if final_norm else x
    yield


N_IN_ARGS = 14
N_OUT_ARGS = 6

PIECE_ORDER = "oiooioioooooiooioo" + "iiii" + "o" * (D_FF // FF_CHUNK) + "iiii" + "o" + "iii"


def _layer_kernel(*refs, tiles_per_seq, final_norm):
    x_in, mod_in, x_out, mod_out = refs[0:4]
    in_args = refs[4:4 + N_IN_ARGS]
    out_args = refs[4 + N_IN_ARGS:4 + N_IN_ARGS + N_OUT_ARGS]
    o_ref = refs[4 + N_IN_ARGS + N_OUT_ARGS]
    hoa, hq, hk, hv, hz, hgc, hgr, ubuf, qbuf, state = refs[5 + N_IN_ARGS + N_OUT_ARGS:]
    hand = (hoa, hq, hk, hv, hz, hgc, hgr)
    t = pl.program_id(0)

    @pl.when(t == 0)
    def _():
        for ref in hand:
            ref[...] = jnp.zeros(ref.shape, ref.dtype)

    @pl.when(t % tiles_per_seq == 0)
    def _():
        ubuf[0:CONV_HALO, :] = jnp.zeros((CONV_HALO, CONV_WIDTH), F32)
        qbuf[0:SC_HALO, :] = jnp.zeros((SC_HALO, 3 * GDN_WIDTH), F32)

    @pl.when(t % tiles_per_seq == 1 % tiles_per_seq)
    def _():
        state[...] = jnp.zeros(state.shape, F32)

    gens = {"o": _stage_out(x_out, mod_out, *out_args, *hand, state, o_ref, final_norm=final_norm),
            "i": _stage_in(x_in, mod_in, *in_args, *hand, ubuf, qbuf)}
    for which in PIECE_ORDER:
        next(gens[which])
    done = object()
    assert all(next(g, done) is done for g in gens.values())


def _layer(x, mod, in_ws, out_ws, final_norm):
    bsz, seq, d = x.shape
    tm = TM
    n = seq // tm
    last = bsz * n - 1
    steps = bsz * n + 1

    def tile_in(t):
        tt = jnp.minimum(t, last)
        return tt // n, tt % n

    def tile_out(t):
        tt = jnp.maximum(t - 1, 0)
        return tt // n, tt % n

    once = lambda a: pl.BlockSpec(a.shape, lambda t: (0,) * a.ndim, pipeline_mode=pl.Buffered(1))
    x_spec = lambda tile: pl.BlockSpec((1, tm, d), lambda t: (*tile(t), 0))
    mod_spec = lambda tile: pl.BlockSpec((1, N_MOD, d), lambda t: (tile(t)[0], 0, 0))
    tri = _chunk_tri_const(tm, GDN_CHUNK).astype(BF16)
    gmat = _block_diag_const(MXU_DIM, CONV_WIDTH // CONV_GROUPS, 1.0 / (CONV_WIDTH // CONV_GROUPS)).astype(BF16)
    (nw, w_ag, w_qkv, w_z, w_ba, cw, cb, gnw, gnb, scw, alog, dtb) = in_ws
    in_full = (nw, w_ag, w_qkv, w_z, w_ba, cw, cb, gnw, gnb, gmat, scw, alog, dtb, tri)
    assert len(in_full) == N_IN_ARGS and len(out_ws) == N_OUT_ARGS
    return pl.pallas_call(
        functools.partial(_layer_kernel, tiles_per_seq=n, final_norm=final_norm),
        out_shape=jax.ShapeDtypeStruct((bsz, seq, d), F32),
        grid=(steps,),
        in_specs=[x_spec(tile_in), mod_spec(tile_in), x_spec(tile_out), mod_spec(tile_out)]
                 + [once(a) for a in in_full] + [once(a) for a in out_ws],
        out_specs=x_spec(tile_out),
        scratch_shapes=[pltpu.VMEM((tm, CONV_WIDTH), BF16),
                        pltpu.VMEM((tm, GDN_WIDTH), F32), pltpu.VMEM((tm, GDN_WIDTH), F32),
                        pltpu.VMEM((tm, GDN_WIDTH), F32), pltpu.VMEM((tm, GDN_WIDTH), F32),
                        pltpu.VMEM((tm, LANES), F32), pltpu.VMEM((SUBLANES, tm), F32),
                        pltpu.VMEM((CONV_HALO + tm, CONV_WIDTH), F32),
                        pltpu.VMEM((SC_HALO + tm, 3 * GDN_WIDTH), F32),
                        pltpu.VMEM((GDN_HEADS, GDN_HEAD_DIM, GDN_HEAD_DIM), F32)],
        compiler_params=pltpu.CompilerParams(dimension_semantics=("arbitrary",),
                                             vmem_limit_bytes=VMEM_LIMIT),
        name="layer",
    )(x, mod, x, mod, *in_full, *out_ws)


def _block_diag_const(n, blk, val):
    idx = np.arange(n) // blk
    return jnp.asarray((idx[:, None] == idx[None, :]).astype(np.float32) * val)


def _chunk_tri_const(n, blk):
    i = np.arange(n)
    m = ((i[:, None] // blk) == (i[None, :] // blk)) & (i[:, None] >= i[None, :])
    return jnp.asarray(m.astype(np.float32))


def _pad_lanes(a, n):
    return jnp.pad(a, [(0, 0)] * (a.ndim - 1) + [(0, n - a.shape[-1])])


def kernel(x, c, w_ada, b_ada, norm_mix_w, w_in, conv_w, conv_b, conv_gn_w, conv_gn_b, gdn_conv_w, gdn_a_log, gdn_dt_bias, gdn_norm_w, w_out, norm_ffn_w, w_ffn_in, w_ffn_out, norm_final_w):
    depth = w_ada.shape[0]
    bsz = x.shape[0]
    for layer in range(depth):
        mod = _adaln(c, w_ada[layer], b_ada[layer]).reshape(bsz, N_MOD, D_MODEL)
        wl = w_in[layer]
        o1 = 2 * CONV_WIDTH
        o2 = o1 + 3 * GDN_WIDTH
        o3 = o2 + GDN_WIDTH
        w_ba = _pad_lanes(wl[:, o3:], LANES).astype(BF16)
        head_row = lambda a: _pad_lanes(jnp.concatenate([jnp.zeros_like(a), a])[None, :], LANES)
        in_ws = (norm_mix_w[layer][None, :], wl[:, :o1].astype(BF16), wl[:, o1:o2].astype(BF16),
                 (0.5 * wl[:, o2:o3]).astype(BF16), w_ba,
                 _pad_lanes(conv_w[layer].T, CONV_HALO).T, conv_b[layer][None, :],
                 conv_gn_w[layer][None, :], conv_gn_b[layer][None, :], gdn_conv_w[layer],
                 head_row(gdn_a_log[layer]), head_row(gdn_dt_bias[layer]))
        gate_half = jnp.where(jnp.arange(2 * D_FF) < D_FF, 0.5, 1.0).astype(F32)
        out_ws = (gdn_norm_w[layer][None, :], w_out[layer].astype(BF16), norm_ffn_w[layer][None, :],
                  (w_ffn_in[layer] * gate_half).astype(BF16), w_ffn_out[layer].astype(BF16),
                  norm_final_w[None, :])
        x = _layer(x, mod, in_ws, out_ws, final_norm=layer == depth - 1)
    return x
```

```python
import functools
import itertools

import jax
import jax.numpy as jnp
import numpy as np
from jax import lax
from jax.experimental import pallas as pl
from jax.experimental.pallas import tpu as pltpu

F32 = jnp.float32
BF16 = jnp.bfloat16

D_MODEL = 1024
CONV_WIDTH = 512
CONV_GROUPS = 8
CONV_KERNEL = 31
GDN_WIDTH = 512
GDN_HEAD_DIM = 128
GDN_HEADS = 4
GDN_SHORT_CONV = 4
GDN_CHUNK = 64
GDN_GROUP = 128
W_IN_SPLITS = (2 * CONV_WIDTH, 2 * CONV_WIDTH + 3 * GDN_WIDTH, 2 * CONV_WIDTH + 4 * GDN_WIDTH)
D_FF = 2816
N_MOD = 6
EPS = 1e-6

LANES = 128
SUBLANES = 8
MXU_DIM = 256

CONV_HALO = 32
SC_HALO = SUBLANES
ADALN_COLS = 2048
TM = 256
FF_CHUNK = D_FF
VMEM_LIMIT = 58 * 1024 * 1024


def _silu_of_twice(h):
    return h + h * jnp.tanh(h)


def _silu(v):
    return _silu_of_twice(0.5 * v)


def _dot(a, b):
    return jnp.dot(a.astype(BF16), b.astype(BF16), preferred_element_type=F32)


def _rms(x, w):
    return x * lax.rsqrt(jnp.mean(x * x, axis=-1, keepdims=True) + EPS) * w


def _rms_mod(x, w, shift, scale):
    return _rms(x, w * (1.0 + scale)) + shift


def _adaln_kernel(c_ref, w_ref, b_ref, o_ref):
    o_ref[...] = _dot(_silu(c_ref[...]), w_ref[...]) + b_ref[...]


def _adaln(c, w, b):
    bsz, d = c.shape
    n = w.shape[1]
    tn = ADALN_COLS
    return pl.pallas_call(
        _adaln_kernel,
        out_shape=jax.ShapeDtypeStruct((bsz, n), F32),
        grid=(n // tn,),
        in_specs=[pl.BlockSpec((bsz, d), lambda j: (0, 0)),
                  pl.BlockSpec((d, tn), lambda j: (0, j)),
                  pl.BlockSpec((1, tn), lambda j: (0, j))],
        out_specs=pl.BlockSpec((bsz, tn), lambda j: (0, j)),
        compiler_params=pltpu.CompilerParams(dimension_semantics=("arbitrary",),
                                             vmem_limit_bytes=VMEM_LIMIT),
        name="adaln",
    )(c, w, b.reshape(1, n))


def _stage_in(x_ref, mod_ref, nw_ref, w_in_ref, w_ba_ref,
              cw_ref, cb_ref, gnw_ref, gnb_ref, gmat_ref, scw_ref, alog_ref, dtb_ref, tri_ref,
              hoa, hq, hk, hv, hz, hgc, hgr, ubuf, qbuf):
    o1, o2, o3 = W_IN_SPLITS
    w_ag_ref, w_qkv_ref, w_z_ref = w_in_ref.at[:, 0:o1], w_in_ref.at[:, o1:o2], w_in_ref.at[:, o2:o3]
    tm = x_ref.shape[1]
    hb = _rms_mod(x_ref[0], nw_ref[...], mod_ref[0, 0:1, :], mod_ref[0, 1:2, :]).astype(BF16)

    ag = jnp.dot(hb, w_ag_ref[...], preferred_element_type=F32)
    ubuf[CONV_HALO:CONV_HALO + tm, :] = ag[:, :CONV_WIDTH] * jax.nn.sigmoid(ag[:, CONV_WIDTH:])
    yield
    qbuf[SC_HALO:SC_HALO + tm, :] = jnp.dot(hb, w_qkv_ref[...], preferred_element_type=F32)
    yield

    z = jnp.dot(hb, w_z_ref[...], preferred_element_type=F32)
    ba = jnp.dot(hb, w_ba_ref[...], preferred_element_type=F32)
    beta = jax.nn.sigmoid(ba)
    al = ba + dtb_ref[...]
    sp = jnp.maximum(al, 0.0) + jnp.log1p(jnp.exp(-jnp.abs(al)))
    g = -jnp.exp(alog_ref[...]) * sp
    gcum = jnp.zeros_like(g)
    rest = g
    for _ in range(3):
        term = rest.astype(BF16)
        gcum = jnp.dot(tri_ref[...], term, preferred_element_type=F32) + gcum
        rest = rest - term.astype(F32)
    lane = lax.broadcasted_iota(jnp.int32, (tm, LANES), 1)
    packed = jnp.where(lane < GDN_HEADS, beta, gcum)
    packed_t = packed.T[0:SUBLANES, :]
    yield
    base = CONV_HALO - (CONV_KERNEL - 1)
    for hp in range(CONV_WIDTH // MXU_DIM):
        half = slice(hp * MXU_DIM, (hp + 1) * MXU_DIM)
        ys = []
        for j in range(hp * MXU_DIM // LANES, (hp + 1) * MXU_DIM // LANES):
            ls = slice(j * LANES, (j + 1) * LANES)
            full = ubuf[:, ls]
            rows = full.shape[0]
            acc = jnp.zeros((tm, LANES), F32) + cb_ref[:, ls]
            for b in range(SUBLANES):
                win = pltpu.roll(full, (rows - base - b) % rows, 0)
                for a in range(-(-(CONV_KERNEL - b) // SUBLANES)):
                    t = SUBLANES * a + b
                    acc = acc + cw_ref[t:t + 1, ls] * win[a * SUBLANES:a * SUBLANES + tm]
            ys.append(acc)
            yield
        y = jnp.concatenate(ys, axis=-1)
        mu = _dot(y, gmat_ref[...])
        d = y - mu
        var = _dot(d * d, gmat_ref[...])
        un_half = d * lax.rsqrt(var + EPS) * (0.5 * gnw_ref[:, half]) + 0.5 * gnb_ref[:, half]
        hoa[:, half] = _silu_of_twice(un_half).astype(hoa.dtype)
        yield
    ubuf[0:CONV_HALO, :] = ubuf[tm:tm + CONV_HALO, :]

    sbase = SC_HALO - (GDN_SHORT_CONV - 1)
    outs = (hq, hk, hv)
    for part in range(3):
        for hp in range(GDN_WIDTH // MXU_DIM):
            ls = slice(part * GDN_WIDTH + hp * MXU_DIM, part * GDN_WIDTH + (hp + 1) * MXU_DIM)
            acc = (0.5 * scw_ref[0:1, ls]) * qbuf[sbase:sbase + tm, ls]
            for t in range(1, GDN_SHORT_CONV):
                acc = acc + (0.5 * scw_ref[t:t + 1, ls]) * qbuf[sbase + t:sbase + t + tm, ls]
            a = _silu_of_twice(acc)
            if part < 2:
                hs = [a[:, j:j + GDN_HEAD_DIM] for j in range(0, MXU_DIM, GDN_HEAD_DIM)]
                a = jnp.concatenate(
                    [h * lax.rsqrt(jnp.sum(h * h, axis=-1, keepdims=True) + EPS) for h in hs], axis=-1)
            outs[part][:, hp * MXU_DIM:(hp + 1) * MXU_DIM] = a
            yield
    qbuf[0:SC_HALO, :] = qbuf[tm:tm + SC_HALO, :]

    hz[...] = z
    hgc[...] = packed
    hgr[...] = packed_t
    yield


def _gdn_heads(hq, hk, hv, hz, hgc, hgr, nw_ref, state):
    rows = hq.shape[0]
    r = GDN_GROUP
    assert r == GDN_HEAD_DIM and rows % r == 0 and r % GDN_CHUNK == 0
    nchunk = rows // GDN_CHUNK
    row = lax.broadcasted_iota(jnp.int32, (r, r), 0)
    col = lax.broadcasted_iota(jnp.int32, (r, r), 1)
    log_c = GDN_CHUNK.bit_length() - 1
    same_chunk = (row >> log_c) == (col >> log_c)
    causal = same_chunk & (row >= col)
    strict = same_chunk & (row > col)
    eye = (row == col).astype(F32)

    gc_all = hgc[...]
    gr_all = hgr[...]
    scale = GDN_HEAD_DIM ** -0.5

    heads = range(GDN_HEADS)
    lanes = [slice(hd * GDN_HEAD_DIM, (hd + 1) * GDN_HEAD_DIM) for hd in heads]
    units = [(slice(g * r, (g + 1) * r), hd) for g in range(rows // r) for hd in heads]
    un = range(len(units))
    q = [hq[rs, lanes[hd]] * scale for rs, hd in units]
    k = [hk[rs, lanes[hd]] for rs, hd in units]
    v = [hv[rs, lanes[hd]] for rs, hd in units]
    gcb = [jnp.broadcast_to(gc_all[rs, GDN_HEADS + hd:GDN_HEADS + hd + 1], (r, GDN_HEAD_DIM))
           for rs, hd in units]
    bb = [jnp.broadcast_to(gc_all[rs, hd:hd + 1], (r, GDN_HEAD_DIM)) for rs, hd in units]
    decay = []
    for u, (rs, hd) in enumerate(units):
        gcj = jnp.broadcast_to(gr_all[GDN_HEADS + hd:GDN_HEADS + hd + 1, rs], (r, r))
        decay.append(jnp.where(causal, jnp.exp(jnp.where(causal, gcb[u] - gcj, 0.0)), 0.0))

    kb = [k[u] * bb[u] for u in un]
    kk = [lax.dot_general(jnp.concatenate([kb[u], q[u]], axis=0).astype(BF16), k[u].astype(BF16),
                          (((1,), (1,)), ((), ())), preferred_element_type=F32) for u in un]
    a_mat = [jnp.where(strict, kk[u][:r] * decay[u], 0.0) for u in un]
    qk = [kk[u][r:] * decay[u] for u in un]

    pair = ((row ^ col) >> 1) == 0
    xinv = [eye - jnp.where(pair, a_mat[u], 0.0) for u in un]
    yield
    s = 2
    while s < GDN_CHUNK:
        ls2 = s.bit_length()
        m = (((row ^ col) >> ls2) == 0) & ((row & s) != 0) & ((col & s) == 0)
        xb = [xinv[u].astype(BF16) for u in un]
        ax = [_dot(jnp.where(m, a_mat[u], 0.0), xb[u]) for u in un]
        xinv = [xinv[u] - _dot(xb[u], ax[u]) for u in un]
        s *= 2
        yield

    egc = [jnp.exp(gcb[u]) for u in un]
    uw = [_dot(xinv[u], jnp.concatenate([v[u] * bb[u], kb[u] * egc[u]], axis=-1)) for u in un]
    qdec = [q[u] * egc[u] for u in un]
    yield

    terms = []
    for c in range(nchunk):
        g, cs = divmod(c * GDN_CHUNK, r)
        rs = slice(cs, cs + GDN_CHUNK)
        last = cs + GDN_CHUNK - 1
        per_head = []
        for hd in heads:
            u = g * GDN_HEADS + hd
            gl = gcb[u][last:last + 1, :]
            kdec = k[u][rs] * jnp.exp(gl - gcb[u][rs])
            kuw = lax.dot_general(kdec.astype(BF16), uw[u][rs].astype(BF16), (((0,), (0,)), ((), ())),
                                  preferred_element_type=F32)
            per_head.append((u, rs, jnp.exp(gl), kuw[:, :GDN_HEAD_DIM], kuw[:, GDN_HEAD_DIM:]))
        terms.append(per_head)
    yield

    st = [state[hd] for hd in heads]
    vnews = [[] for _ in un]
    qss = [[] for _ in un]
    for per_head in terms:
        prods = [_dot(jnp.concatenate([kw, uw[u][rs, GDN_HEAD_DIM:], qdec[u][rs]], axis=0), st[hd])
                 for hd, (u, rs, _, _, kw) in zip(heads, per_head)]
        for hd, (u, rs, e, ku, _) in zip(heads, per_head):
            vnews[u].append(uw[u][rs, :GDN_HEAD_DIM] - prods[hd][GDN_HEAD_DIM:GDN_HEAD_DIM + GDN_CHUNK])
            qss[u].append(prods[hd][GDN_HEAD_DIM + GDN_CHUNK:])
            st[hd] = st[hd] * e + ku - prods[hd][:GDN_HEAD_DIM]
        yield
    for hd in heads:
        state[hd] = st[hd]
    outs = []
    for hd in heads:
        o = jnp.concatenate([jnp.concatenate(qss[u], axis=0) + _dot(qk[u], jnp.concatenate(vnews[u], axis=0))
                             for u in range(hd, len(units), GDN_HEADS)], axis=0)
        o = o * lax.rsqrt(jnp.mean(o * o, axis=-1, keepdims=True) + EPS) * nw_ref[...]
        outs.append((o * _silu_of_twice(hz[:, lanes[hd]])).astype(BF16))
    return outs


def _stage_out(x_ref, mod_ref, gnw_ref, wo_ref, nfw_ref, wfi_ref, wfo_ref, nlw_ref,
               hoa, hq, hk, hv, hz, hgc, hgr, state, o_ref, *, final_norm):
    mix = jnp.dot(hoa[...], wo_ref[0:CONV_WIDTH, :], preferred_element_type=F32)
    ob = yield from _gdn_heads(hq, hk, hv, hz, hgc, hgr, gnw_ref, state)
    mix = jnp.dot(jnp.concatenate(ob, axis=-1), wo_ref[CONV_WIDTH:, :], preferred_element_type=F32) + mix
    x = x_ref[0] + mod_ref[0, 2:3, :] * mix
    hb = _rms_mod(x, nfw_ref[...], mod_ref[0, 3:4, :], mod_ref[0, 4:5, :]).astype(BF16)
    yield
    ffn = None
    for j in range(D_FF // FF_CHUNK):
        cs = slice(j * FF_CHUNK, (j + 1) * FF_CHUNK)
        gate_half = jnp.dot(hb, wfi_ref[:, cs], preferred_element_type=F32).astype(BF16)
        up = jnp.dot(hb, wfi_ref[:, D_FF + j * FF_CHUNK:D_FF + (j + 1) * FF_CHUNK],
                     preferred_element_type=F32).astype(BF16)
        part = jnp.dot(_silu_of_twice(gate_half) * up, wfo_ref[cs, :], preferred_element_type=F32)
        ffn = part if ffn is None else part + ffn
        yield
    x = x + mod_ref[0, 5:6, :] * ffn
    o_ref[0] = _rms(x, nlw_ref[...]) if final_norm else x
    yield


N_IN_ARGS = 12
N_OUT_ARGS = 6

PIECE_ORDER = "oiooioioooooiooioo" + "iiii" + "o" * (D_FF // FF_CHUNK) + "iiii" + "o" + "iii"


def _layer_kernel(*refs, tiles_per_seq, final_norm):
    x_in, mod_in, x_out, mod_out = refs[0:4]
    in_args = refs[4:4 + N_IN_ARGS]
    out_args = refs[4 + N_IN_ARGS:4 + N_IN_ARGS + N_OUT_ARGS]
    o_ref = refs[4 + N_IN_ARGS + N_OUT_ARGS]
    hoa, hq, hk, hv, hz, hgc, hgr, ubuf, qbuf, state = refs[5 + N_IN_ARGS + N_OUT_ARGS:]
    hand = (hoa, hq, hk, hv, hz, hgc, hgr)
    t = pl.program_id(0)

    @pl.when(t == 0)
    def _():
        for ref in hand:
            ref[...] = jnp.zeros(ref.shape, ref.dtype)

    @pl.when(t % tiles_per_seq == 0)
    def _():
        ubuf[0:CONV_HALO, :] = jnp.zeros((CONV_HALO, CONV_WIDTH), F32)
        qbuf[0:SC_HALO, :] = jnp.zeros((SC_HALO, 3 * GDN_WIDTH), F32)

    @pl.when(t % tiles_per_seq == 1 % tiles_per_seq)
    def _():
        state[...] = jnp.zeros(state.shape, F32)

    gens = {"o": _stage_out(x_out, mod_out, *out_args, *hand, state, o_ref, final_norm=final_norm),
            "i": _stage_in(x_in, mod_in, *in_args, *hand, ubuf, qbuf)}
    for which in PIECE_ORDER:
        next(gens[which])
    done = object()
    assert all(next(g, done) is done for g in gens.values())


def _layer(x, mod, in_ws, out_ws, final_norm):
    bsz, seq, d = x.shape
    tm = TM
    n = seq // tm
    last = bsz * n - 1
    steps = bsz * n + 1

    def tile_in(t):
        tt = jnp.minimum(t, last)
        return tt // n, tt % n

    def tile_out(t):
        tt = jnp.maximum(t - 1, 0)
        return tt // n, tt % n

    once = lambda a: pl.BlockSpec(a.shape, lambda t: (0,) * a.ndim, pipeline_mode=pl.Buffered(1))
    x_spec = lambda tile: pl.BlockSpec((1, tm, d), lambda t: (*tile(t), 0))
    mod_spec = lambda tile: pl.BlockSpec((1, N_MOD, d), lambda t: (tile(t)[0], 0, 0))
    tri = _chunk_tri_const(tm, GDN_CHUNK).astype(BF16)
    gmat = _block_diag_const(MXU_DIM, CONV_WIDTH // CONV_GROUPS, 1.0 / (CONV_WIDTH // CONV_GROUPS)).astype(BF16)
    (nw, w_in, w_ba, cw, cb, gnw, gnb, scw, alog, dtb) = in_ws
    in_full = (nw, w_in, w_ba, cw, cb, gnw, gnb, gmat, scw, alog, dtb, tri)
    assert len(in_full) == N_IN_ARGS and len(out_ws) == N_OUT_ARGS
    return pl.pallas_call(
        functools.partial(_layer_kernel, tiles_per_seq=n, final_norm=final_norm),
        out_shape=jax.ShapeDtypeStruct((bsz, seq, d), F32),
        grid=(steps,),
        in_specs=[x_spec(tile_in), mod_spec(tile_in), x_spec(tile_out), mod_spec(tile_out)]
                 + [once(a) for a in in_full] + [once(a) for a in out_ws],
        out_specs=x_spec(tile_out),
        scratch_shapes=[pltpu.VMEM((tm, CONV_WIDTH), BF16),
                        pltpu.VMEM((tm, GDN_WIDTH), F32), pltpu.VMEM((tm, GDN_WIDTH), F32),
                        pltpu.VMEM((tm, GDN_WIDTH), F32), pltpu.VMEM((tm, GDN_WIDTH), F32),
                        pltpu.VMEM((tm, LANES), F32), pltpu.VMEM((SUBLANES, tm), F32),
                        pltpu.VMEM((CONV_HALO + tm, CONV_WIDTH), F32),
                        pltpu.VMEM((SC_HALO + tm, 3 * GDN_WIDTH), F32),
                        pltpu.VMEM((GDN_HEADS, GDN_HEAD_DIM, GDN_HEAD_DIM), F32)],
        compiler_params=pltpu.CompilerParams(dimension_semantics=("arbitrary",),
                                             vmem_limit_bytes=VMEM_LIMIT),
        name="layer",
    )(x, mod, x, mod, *in_full, *out_ws)


def _block_diag_const(n, blk, val):
    idx = np.arange(n) // blk
    return jnp.asarray((idx[:, None] == idx[None, :]).astype(np.float32) * val)


def _chunk_tri_const(n, blk):
    i = np.arange(n)
    m = ((i[:, None] // blk) == (i[None, :] // blk)) & (i[:, None] >= i[None, :])
    return jnp.asarray(m.astype(np.float32))


def _pad_lanes(a, n):
    return jnp.pad(a, [(0, 0)] * (a.ndim - 1) + [(0, n - a.shape[-1])])


def kernel(x, c, w_ada, b_ada, norm_mix_w, w_in, conv_w, conv_b, conv_gn_w, conv_gn_b, gdn_conv_w, gdn_a_log, gdn_dt_bias, gdn_norm_w, w_out, norm_ffn_w, w_ffn_in, w_ffn_out, norm_final_w):
    depth = w_ada.shape[0]
    bsz = x.shape[0]
    for layer in range(depth):
        mod = _adaln(c, w_ada[layer], b_ada[layer]).reshape(bsz, N_MOD, D_MODEL)
        o1, o2, o3 = W_IN_SPLITS
        cols = jnp.arange(w_in.shape[-1])
        z_half = jnp.where((cols >= o2) & (cols < o3), 0.5, 1.0).astype(F32)
        w_ba = _pad_lanes(w_in[layer][:, o3:], LANES).astype(BF16)
        head_row = lambda a: _pad_lanes(jnp.concatenate([jnp.zeros_like(a), a])[None, :], LANES)
        in_ws = (norm_mix_w[layer][None, :], (w_in[layer] * z_half).astype(BF16), w_ba,
                 _pad_lanes(conv_w[layer].T, CONV_HALO).T, conv_b[layer][None, :],
                 conv_gn_w[layer][None, :], conv_gn_b[layer][None, :], gdn_conv_w[layer],
                 head_row(gdn_a_log[layer]), head_row(gdn_dt_bias[layer]))
        gate_half = jnp.where(jnp.arange(2 * D_FF) < D_FF, 0.5, 1.0).astype(F32)
        out_ws = (gdn_norm_w[layer][None, :], w_out[layer].astype(BF16), norm_ffn_w[layer][None, :],
                  (w_ffn_in[layer] * gate_half).astype(BF16), w_ffn_out[layer].astype(BF16),
                  norm_final_w[None, :])
        x = _layer(x, mod, in_ws, out_ws, final_norm=layer == depth - 1)
    return x
```

```python
import functools
import itertools

import jax
import jax.numpy as jnp
import numpy as np
from jax import lax
from jax.experimental import pallas as pl
from jax.experimental.pallas import tpu as pltpu

F32 = jnp.float32
BF16 = jnp.bfloat16

D_MODEL = 1024
CONV_WIDTH = 512
CONV_GROUPS = 8
CONV_KERNEL = 31
GDN_WIDTH = 512
GDN_HEAD_DIM = 128
GDN_HEADS = 4
GDN_SHORT_CONV = 4
GDN_CHUNK = 64
GDN_GROUP = 128
W_IN_SPLITS = (2 * CONV_WIDTH, 2 * CONV_WIDTH + 3 * GDN_WIDTH, 2 * CONV_WIDTH + 4 * GDN_WIDTH)
D_FF = 2816
N_MOD = 6
EPS = 1e-6

LANES = 128
SUBLANES = 8
MXU_DIM = 256

CONV_HALO = 32
SC_HALO = SUBLANES
ADALN_COLS = 2048
TM = 256
FF_CHUNK = D_FF
VMEM_LIMIT = 58 * 1024 * 1024


def _silu_of_twice(h):
    return h + h * jnp.tanh(h)


def _silu(v):
    return _silu_of_twice(0.5 * v)


def _dot(a, b):
    return jnp.dot(a.astype(BF16), b.astype(BF16), preferred_element_type=F32)


def _rms(x, w):
    return x * lax.rsqrt(jnp.mean(x * x, axis=-1, keepdims=True) + EPS) * w


def _rms_mod(x, w, shift, scale):
    return _rms(x, w * (1.0 + scale)) + shift


def _adaln_kernel(c_ref, w_ref, b_ref, o_ref):
    o_ref[...] = _dot(_silu(c_ref[...]), w_ref[...]) + b_ref[...]


def _adaln(c, w, b):
    bsz, d = c.shape
    n = w.shape[1]
    tn = ADALN_COLS
    return pl.pallas_call(
        _adaln_kernel,
        out_shape=jax.ShapeDtypeStruct((bsz, n), F32),
        grid=(n // tn,),
        in_specs=[pl.BlockSpec((bsz, d), lambda j: (0, 0)),
                  pl.BlockSpec((d, tn), lambda j: (0, j)),
                  pl.BlockSpec((1, tn), lambda j: (0, j))],
        out_specs=pl.BlockSpec((bsz, tn), lambda j: (0, j)),
        compiler_params=pltpu.CompilerParams(dimension_semantics=("arbitrary",),
                                             vmem_limit_bytes=VMEM_LIMIT),
        name="adaln",
    )(c, w, b.reshape(1, n))


def _store_lane_tiles(buf, tile, halo):
    tm = tile.shape[0]
    for j in range(tile.shape[1] // LANES):
        buf[j, halo:halo + tm, :] = tile[:, j * LANES:(j + 1) * LANES]


def _stage_in(x_ref, mod_ref, nw_ref, w_in_ref, w_ba_ref,
              cw_ref, cb_ref, gnw_ref, gnb_ref, gmat_ref, scw_ref, alog_ref, dtb_ref, tri_ref,
              hoa, hq, hk, hv, hz, hgc, hgr, ubuf, qbuf):
    o1, o2, o3 = W_IN_SPLITS
    w_ag_ref, w_qkv_ref, w_z_ref = w_in_ref.at[:, 0:o1], w_in_ref.at[:, o1:o2], w_in_ref.at[:, o2:o3]
    tm = x_ref.shape[1]
    hb = _rms_mod(x_ref[0], nw_ref[...], mod_ref[0, 0:1, :], mod_ref[0, 1:2, :]).astype(BF16)

    ag = jnp.dot(hb, w_ag_ref[...], preferred_element_type=F32)
    _store_lane_tiles(ubuf, ag[:, :CONV_WIDTH] * jax.nn.sigmoid(ag[:, CONV_WIDTH:]), CONV_HALO)
    yield
    _store_lane_tiles(qbuf, jnp.dot(hb, w_qkv_ref[...], preferred_element_type=F32), SC_HALO)
    yield

    z = jnp.dot(hb, w_z_ref[...], preferred_element_type=F32)
    ba = jnp.dot(hb, w_ba_ref[...], preferred_element_type=F32)
    beta = jax.nn.sigmoid(ba)
    al = ba + dtb_ref[...]
    sp = jnp.maximum(al, 0.0) + jnp.log1p(jnp.exp(-jnp.abs(al)))
    g = -jnp.exp(alog_ref[...]) * sp
    gcum = jnp.zeros_like(g)
    rest = g
    for _ in range(3):
        term = rest.astype(BF16)
        gcum = jnp.dot(tri_ref[...], term, preferred_element_type=F32) + gcum
        rest = rest - term.astype(F32)
    lane = lax.broadcasted_iota(jnp.int32, (tm, LANES), 1)
    packed = jnp.where(lane < GDN_HEADS, beta, gcum)
    packed_t = packed.T[0:SUBLANES, :]
    yield
    base = CONV_HALO - (CONV_KERNEL - 1)
    for hp in range(CONV_WIDTH // MXU_DIM):
        half = slice(hp * MXU_DIM, (hp + 1) * MXU_DIM)
        ys = []
        for j in range(hp * MXU_DIM // LANES, (hp + 1) * MXU_DIM // LANES):
            ls = slice(j * LANES, (j + 1) * LANES)
            acc = jnp.zeros((tm, LANES), F32) + cb_ref[:, ls]
            for t in range(CONV_KERNEL):
                acc = acc + cw_ref[t:t + 1, ls] * ubuf[j, base + t:base + t + tm, :]
            ys.append(acc)
            yield
        y = jnp.concatenate(ys, axis=-1)
        mu = _dot(y, gmat_ref[...])
        d = y - mu
        var = _dot(d * d, gmat_ref[...])
        un_half = d * lax.rsqrt(var + EPS) * (0.5 * gnw_ref[:, half]) + 0.5 * gnb_ref[:, half]
        hoa[:, half] = _silu_of_twice(un_half).astype(hoa.dtype)
        yield
    ubuf[:, 0:CONV_HALO, :] = ubuf[:, tm:tm + CONV_HALO, :]

    sbase = SC_HALO - (GDN_SHORT_CONV - 1)
    outs = (hq, hk, hv)
    for part in range(3):
        for hp in range(GDN_WIDTH // MXU_DIM):
            heads_out = []
            for j in range((part * GDN_WIDTH + hp * MXU_DIM) // LANES, (part * GDN_WIDTH + (hp + 1) * MXU_DIM) // LANES):
                ls = slice(j * LANES, (j + 1) * LANES)
                acc = (0.5 * scw_ref[0:1, ls]) * qbuf[j, sbase:sbase + tm, :]
                for t in range(1, GDN_SHORT_CONV):
                    acc = acc + (0.5 * scw_ref[t:t + 1, ls]) * qbuf[j, sbase + t:sbase + t + tm, :]
                a = _silu_of_twice(acc)
                if part < 2:
                    a = a * lax.rsqrt(jnp.sum(a * a, axis=-1, keepdims=True) + EPS)
                heads_out.append(a)
            outs[part][:, hp * MXU_DIM:(hp + 1) * MXU_DIM] = jnp.concatenate(heads_out, axis=-1)
            yield
    qbuf[:, 0:SC_HALO, :] = qbuf[:, tm:tm + SC_HALO, :]

    hz[...] = z
    hgc[...] = packed
    hgr[...] = packed_t
    yield


def _gdn_heads(hq, hk, hv, hz, hgc, hgr, nw_ref, state):
    rows = hq.shape[0]
    r = GDN_GROUP
    assert r == GDN_HEAD_DIM and rows % r == 0 and r % GDN_CHUNK == 0
    nchunk = rows // GDN_CHUNK
    row = lax.broadcasted_iota(jnp.int32, (r, r), 0)
    col = lax.broadcasted_iota(jnp.int32, (r, r), 1)
    log_c = GDN_CHUNK.bit_length() - 1
    same_chunk = (row >> log_c) == (col >> log_c)
    causal = same_chunk & (row >= col)
    strict = same_chunk & (row > col)
    eye = (row == col).astype(F32)

    gc_all = hgc[...]
    gr_all = hgr[...]
    scale = GDN_HEAD_DIM ** -0.5

    heads = range(GDN_HEADS)
    lanes = [slice(hd * GDN_HEAD_DIM, (hd + 1) * GDN_HEAD_DIM) for hd in heads]
    units = [(slice(g * r, (g + 1) * r), hd) for g in range(rows // r) for hd in heads]
    un = range(len(units))
    q = [hq[rs, lanes[hd]] * scale for rs, hd in units]
    k = [hk[rs, lanes[hd]] for rs, hd in units]
    v = [hv[rs, lanes[hd]] for rs, hd in units]
    gcb = [jnp.broadcast_to(gc_all[rs, GDN_HEADS + hd:GDN_HEADS + hd + 1], (r, GDN_HEAD_DIM))
           for rs, hd in units]
    bb = [jnp.broadcast_to(gc_all[rs, hd:hd + 1], (r, GDN_HEAD_DIM)) for rs, hd in units]
    decay = []
    for u, (rs, hd) in enumerate(units):
        gcj = jnp.broadcast_to(gr_all[GDN_HEADS + hd:GDN_HEADS + hd + 1, rs], (r, r))
        decay.append(jnp.where(causal, jnp.exp(jnp.where(causal, gcb[u] - gcj, 0.0)), 0.0))

    kb = [k[u] * bb[u] for u in un]
    kk = [lax.dot_general(jnp.concatenate([kb[u], q[u]], axis=0).astype(BF16), k[u].astype(BF16),
                          (((1,), (1,)), ((), ())), preferred_element_type=F32) for u in un]
    a_mat = [jnp.where(strict, kk[u][:r] * decay[u], 0.0) for u in un]
    qk = [kk[u][r:] * decay[u] for u in un]

    pair = ((row ^ col) >> 1) == 0
    xinv = [eye - jnp.where(pair, a_mat[u], 0.0) for u in un]
    yield
    s = 2
    while s < GDN_CHUNK:
        ls2 = s.bit_length()
        m = (((row ^ col) >> ls2) == 0) & ((row & s) != 0) & ((col & s) == 0)
        xb = [xinv[u].astype(BF16) for u in un]
        ax = [_dot(jnp.where(m, a_mat[u], 0.0), xb[u]) for u in un]
        xinv = [xinv[u] - _dot(xb[u], ax[u]) for u in un]
        s *= 2
        yield

    egc = [jnp.exp(gcb[u]) for u in un]
    uw = [_dot(xinv[u], jnp.concatenate([v[u] * bb[u], kb[u] * egc[u]], axis=-1)) for u in un]
    qdec = [q[u] * egc[u] for u in un]
    yield

    terms = []
    for c in range(nchunk):
        g, cs = divmod(c * GDN_CHUNK, r)
        rs = slice(cs, cs + GDN_CHUNK)
        last = cs + GDN_CHUNK - 1
        per_head = []
        for hd in heads:
            u = g * GDN_HEADS + hd
            gl = gcb[u][last:last + 1, :]
            kdec = k[u][rs] * jnp.exp(gl - gcb[u][rs])
            kuw = lax.dot_general(kdec.astype(BF16), uw[u][rs].astype(BF16), (((0,), (0,)), ((), ())),
                                  preferred_element_type=F32)
            per_head.append((u, rs, jnp.exp(gl), kuw[:, :GDN_HEAD_DIM], kuw[:, GDN_HEAD_DIM:]))
        terms.append(per_head)
    yield

    st = [state[hd] for hd in heads]
    vnews = [[] for _ in un]
    qss = [[] for _ in un]
    for per_head in terms:
        prods = [_dot(jnp.concatenate([kw, uw[u][rs, GDN_HEAD_DIM:], qdec[u][rs]], axis=0), st[hd])
                 for hd, (u, rs, _, _, kw) in zip(heads, per_head)]
        for hd, (u, rs, e, ku, _) in zip(heads, per_head):
            vnews[u].append(uw[u][rs, :GDN_HEAD_DIM] - prods[hd][GDN_HEAD_DIM:GDN_HEAD_DIM + GDN_CHUNK])
            qss[u].append(prods[hd][GDN_HEAD_DIM + GDN_CHUNK:])
            st[hd] = st[hd] * e + ku - prods[hd][:GDN_HEAD_DIM]
        yield
    for hd in heads:
        state[hd] = st[hd]
    outs = []
    for hd in heads:
        o = jnp.concatenate([jnp.concatenate(qss[u], axis=0) + _dot(qk[u], jnp.concatenate(vnews[u], axis=0))
                             for u in range(hd, len(units), GDN_HEADS)], axis=0)
        o = o * lax.rsqrt(jnp.mean(o * o, axis=-1, keepdims=True) + EPS) * nw_ref[...]
        outs.append((o * _silu_of_twice(hz[:, lanes[hd]])).astype(BF16))
    return outs


def _stage_out(x_ref, mod_ref, gnw_ref, wo_ref, nfw_ref, wfi_ref, wfo_ref, nlw_ref,
               hoa, hq, hk, hv, hz, hgc, hgr, state, o_ref, *, final_norm):
    mix = jnp.dot(hoa[...], wo_ref[0:CONV_WIDTH, :], preferred_element_type=F32)
    ob = yield from _gdn_heads(hq, hk, hv, hz, hgc, hgr, gnw_ref, state)
    mix = jnp.dot(jnp.concatenate(ob, axis=-1), wo_ref[CONV_WIDTH:, :], preferred_element_type=F32) + mix
    x = x_ref[0] + mod_ref[0, 2:3, :] * mix
    hb = _rms_mod(x, nfw_ref[...], mod_ref[0, 3:4, :], mod_ref[0, 4:5, :]).astype(BF16)
    yield
    ffn = None
    for j in range(D_FF // FF_CHUNK):
        cs = slice(j * FF_CHUNK, (j + 1) * FF_CHUNK)
        gate_half = jnp.dot(hb, wfi_ref[:, cs], preferred_element_type=F32).astype(BF16)
        up = jnp.dot(hb, wfi_ref[:, D_FF + j * FF_CHUNK:D_FF + (j + 1) * FF_CHUNK],
                     preferred_element_type=F32).astype(BF16)
        part = jnp.dot(_silu_of_twice(gate_half) * up, wfo_ref[cs, :], preferred_element_type=F32)
        ffn = part if ffn is None else part + ffn
        yield
    x = x + mod_ref[0, 5:6, :] * ffn
    o_ref[0] = _rms(x, nlw_ref[...]) if final_norm else x
    yield


N_IN_ARGS = 12
N_OUT_ARGS = 6

PIECE_ORDER = "oiooioioooooiooioo" + "iiii" + "o" * (D_FF // FF_CHUNK) + "iiii" + "o" + "iii"


def _layer_kernel(*refs, tiles_per_seq, final_norm):
    x_in, mod_in, x_out, mod_out = refs[0:4]
    in_args = refs[4:4 + N_IN_ARGS]
    out_args = refs[4 + N_IN_ARGS:4 + N_IN_ARGS + N_OUT_ARGS]
    o_ref = refs[4 + N_IN_ARGS + N_OUT_ARGS]
    hoa, hq, hk, hv, hz, hgc, hgr, ubuf, qbuf, state = refs[5 + N_IN_ARGS + N_OUT_ARGS:]
    hand = (hoa, hq, hk, hv, hz, hgc, hgr)
    t = pl.program_id(0)

    @pl.when(t == 0)
    def _():
        for ref in hand:
            ref[...] = jnp.zeros(ref.shape, ref.dtype)

    @pl.when(t % tiles_per_seq == 0)
    def _():
        ubuf[:, 0:CONV_HALO, :] = jnp.zeros((ubuf.shape[0], CONV_HALO, LANES), F32)
        qbuf[:, 0:SC_HALO, :] = jnp.zeros((qbuf.shape[0], SC_HALO, LANES), F32)

    @pl.when(t % tiles_per_seq == 1 % tiles_per_seq)
    def _():
        state[...] = jnp.zeros(state.shape, F32)

    gens = {"o": _stage_out(x_out, mod_out, *out_args, *hand, state, o_ref, final_norm=final_norm),
            "i": _stage_in(x_in, mod_in, *in_args, *hand, ubuf, qbuf)}
    for which in PIECE_ORDER:
        next(gens[which])
    done = object()
    assert all(next(g, done) is done for g in gens.values())


def _layer(x, mod, in_ws, out_ws, final_norm):
    bsz, seq, d = x.shape
    tm = TM
    n = seq // tm
    last = bsz * n - 1
    steps = bsz * n + 1

    def tile_in(t):
        tt = jnp.minimum(t, last)
        return tt // n, tt % n

    def tile_out(t):
        tt = jnp.maximum(t - 1, 0)
        return tt // n, tt % n

    once = lambda a: pl.BlockSpec(a.shape, lambda t: (0,) * a.ndim, pipeline_mode=pl.Buffered(1))
    x_spec = lambda tile: pl.BlockSpec((1, tm, d), lambda t: (*tile(t), 0))
    mod_spec = lambda tile: pl.BlockSpec((1, N_MOD, d), lambda t: (tile(t)[0], 0, 0))
    tri = _chunk_tri_const(tm, GDN_CHUNK).astype(BF16)
    gmat = _block_diag_const(MXU_DIM, CONV_WIDTH // CONV_GROUPS, 1.0 / (CONV_WIDTH // CONV_GROUPS)).astype(BF16)
    (nw, w_in, w_ba, cw, cb, gnw, gnb, scw, alog, dtb) = in_ws
    in_full = (nw, w_in, w_ba, cw, cb, gnw, gnb, gmat, scw, alog, dtb, tri)
    assert len(in_full) == N_IN_ARGS and len(out_ws) == N_OUT_ARGS
    return pl.pallas_call(
        functools.partial(_layer_kernel, tiles_per_seq=n, final_norm=final_norm),
        out_shape=jax.ShapeDtypeStruct((bsz, seq, d), F32),
        grid=(steps,),
        in_specs=[x_spec(tile_in), mod_spec(tile_in), x_spec(tile_out), mod_spec(tile_out)]
                 + [once(a) for a in in_full] + [once(a) for a in out_ws],
        out_specs=x_spec(tile_out),
        scratch_shapes=[pltpu.VMEM((tm, CONV_WIDTH), BF16),
                        pltpu.VMEM((tm, GDN_WIDTH), F32), pltpu.VMEM((tm, GDN_WIDTH), F32),
                        pltpu.VMEM((tm, GDN_WIDTH), F32), pltpu.VMEM((tm, GDN_WIDTH), F32),
                        pltpu.VMEM((tm, LANES), F32), pltpu.VMEM((SUBLANES, tm), F32),
                        pltpu.VMEM((CONV_WIDTH // LANES, CONV_HALO + tm, LANES), F32),
                        pltpu.VMEM((3 * GDN_WIDTH // LANES, SC_HALO + tm, LANES), F32),
                        pltpu.VMEM((GDN_HEADS, GDN_HEAD_DIM, GDN_HEAD_DIM), F32)],
        compiler_params=pltpu.CompilerParams(dimension_semantics=("arbitrary",),
                                             vmem_limit_bytes=VMEM_LIMIT),
        name="layer",
    )(x, mod, x, mod, *in_full, *out_ws)


def _block_diag_const(n, blk, val):
    idx = np.arange(n) // blk
    return jnp.asarray((idx[:, None] == idx[None, :]).astype(np.float32) * val)


def _chunk_tri_const(n, blk):
    i = np.arange(n)
    m = ((i[:, None] // blk) == (i[None, :] // blk)) & (i[:, None] >= i[None, :])
    return jnp.asarray(m.astype(np.float32))


def _pad_lanes(a, n):
    return jnp.pad(a, [(0, 0)] * (a.ndim - 1) + [(0, n - a.shape[-1])])


def kernel(x, c, w_ada, b_ada, norm_mix_w, w_in, conv_w, conv_b, conv_gn_w, conv_gn_b, gdn_conv_w, gdn_a_log, gdn_dt_bias, gdn_norm_w, w_out, norm_ffn_w, w_ffn_in, w_ffn_out, norm_final_w):
    depth = w_ada.shape[0]
    bsz = x.shape[0]
    for layer in range(depth):
        mod = _adaln(c, w_ada[layer], b_ada[layer]).reshape(bsz, N_MOD, D_MODEL)
        o1, o2, o3 = W_IN_SPLITS
        cols = jnp.arange(w_in.shape[-1])
        z_half = jnp.where((cols >= o2) & (cols < o3), 0.5, 1.0).astype(F32)
        w_ba = _pad_lanes(w_in[layer][:, o3:], LANES).astype(BF16)
        head_row = lambda a: _pad_lanes(jnp.concatenate([jnp.zeros_like(a), a])[None, :], LANES)
        in_ws = (norm_mix_w[layer][None, :], (w_in[layer] * z_half).astype(BF16), w_ba,
                 _pad_lanes(conv_w[layer].T, CONV_HALO).T, conv_b[layer][None, :],
                 conv_gn_w[layer][None, :], conv_gn_b[layer][None, :], gdn_conv_w[layer],
                 head_row(gdn_a_log[layer]), head_row(gdn_dt_bias[layer]))
        gate_half = jnp.where(jnp.arange(2 * D_FF) < D_FF, 0.5, 1.0).astype(F32)
        out_ws = (gdn_norm_w[layer][None, :], w_out[layer].astype(BF16), norm_ffn_w[layer][None, :],
                  (w_ffn_in[layer] * gate_half).astype(BF16), w_ffn_out[layer].astype(BF16),
                  norm_final_w[None, :])
        x = _layer(x, mod, in_ws, out_ws, final_norm=layer == depth - 1)
    return x
```

```python
import functools

import jax
import jax.numpy as jnp
import numpy as np
from jax import lax
from jax.experimental import pallas as pl
from jax.experimental.pallas import tpu as pltpu

F32 = jnp.float32
BF16 = jnp.bfloat16

D_MODEL = 1024
CONV_WIDTH = 512
CONV_GROUPS = 8
CONV_KERNEL = 31
GDN_WIDTH = 512
GDN_HEAD_DIM = 128
GDN_HEADS = 4
GDN_SHORT_CONV = 4
GDN_CHUNK = 64
GDN_GROUP = 128
W_IN_SPLITS = (2 * CONV_WIDTH, 2 * CONV_WIDTH + 3 * GDN_WIDTH, 2 * CONV_WIDTH + 4 * GDN_WIDTH)
D_FF = 2816
N_MOD = 6
EPS = 1e-6

LANES = 128
SUBLANES = 8
MXU_DIM = 256

CONV_HALO = 32
SC_HALO = SUBLANES
ADALN_COLS = 2048
TM = 256
FF_CHUNK = D_FF
VMEM_LIMIT = 58 * 1024 * 1024


def _silu_of_twice(h):
    return h + h * jnp.tanh(h)


def _silu(v):
    return _silu_of_twice(0.5 * v)


def _dot(a, b):
    return jnp.dot(a.astype(BF16), b.astype(BF16), preferred_element_type=F32)


def _rms(x, w):
    return x * lax.rsqrt(jnp.mean(x * x, axis=-1, keepdims=True) + EPS) * w


def _rms_mod(x, w, shift, scale):
    return _rms(x, w * (1.0 + scale)) + shift


def _adaln_kernel(c_ref, w_ref, b_ref, o_ref):
    o_ref[...] = _dot(_silu(c_ref[...]), w_ref[...]) + b_ref[...]


def _adaln(c, w, b):
    bsz, d = c.shape
    n = w.shape[1]
    tn = ADALN_COLS
    return pl.pallas_call(
        _adaln_kernel,
        out_shape=jax.ShapeDtypeStruct((bsz, n), F32),
        grid=(n // tn,),
        in_specs=[pl.BlockSpec((bsz, d), lambda j: (0, 0)),
                  pl.BlockSpec((d, tn), lambda j: (0, j)),
                  pl.BlockSpec((1, tn), lambda j: (0, j))],
        out_specs=pl.BlockSpec((bsz, tn), lambda j: (0, j)),
        compiler_params=pltpu.CompilerParams(dimension_semantics=("arbitrary",),
                                             vmem_limit_bytes=VMEM_LIMIT),
        name="adaln",
    )(c, w, b.reshape(1, n))


def _store_lane_tiles(buf, tile, halo):
    tm = tile.shape[0]
    for j in range(tile.shape[1] // LANES):
        buf[j, halo:halo + tm, :] = tile[:, j * LANES:(j + 1) * LANES]


def _stage_in(x_ref, mod_ref, nw_ref, w_in_ref, w_ba_ref,
              cw_ref, cb_ref, gnw_ref, gnb_ref, gmat_ref, scw_ref, alog_ref, dtb_ref, tri_ref,
              hoa, hq, hk, hv, hz, hgc, hgr, ubuf, qbuf):
    o1, o2, o3 = W_IN_SPLITS
    w_ag_ref, w_qkv_ref, w_z_ref = w_in_ref.at[:, 0:o1], w_in_ref.at[:, o1:o2], w_in_ref.at[:, o2:o3]
    tm = x_ref.shape[1]
    hb = _rms_mod(x_ref[0], nw_ref[...], mod_ref[0, 0:1, :], mod_ref[0, 1:2, :]).astype(BF16)

    ag = jnp.dot(hb, w_ag_ref[...], preferred_element_type=F32)
    _store_lane_tiles(ubuf, ag[:, :CONV_WIDTH] * jax.nn.sigmoid(ag[:, CONV_WIDTH:]), CONV_HALO)
    yield
    _store_lane_tiles(qbuf, jnp.dot(hb, w_qkv_ref[...], preferred_element_type=F32), SC_HALO)
    yield

    z = jnp.dot(hb, w_z_ref[...], preferred_element_type=F32)
    ba = jnp.dot(hb, w_ba_ref[...], preferred_element_type=F32)
    beta = jax.nn.sigmoid(ba)
    al = ba + dtb_ref[...]
    sp = jnp.maximum(al, 0.0) + jnp.log1p(jnp.exp(-jnp.abs(al)))
    g = -jnp.exp(alog_ref[...]) * sp
    gcum = jnp.zeros_like(g)
    rest = g
    for _ in range(3):
        term = rest.astype(BF16)
        gcum = jnp.dot(tri_ref[...], term, preferred_element_type=F32) + gcum
        rest = rest - term.astype(F32)
    lane = lax.broadcasted_iota(jnp.int32, (tm, LANES), 1)
    packed = jnp.where(lane < GDN_HEADS, beta, gcum)
    packed_t = packed.T[0:SUBLANES, :]
    yield
    base = CONV_HALO - (CONV_KERNEL - 1)
    for hp in range(CONV_WIDTH // MXU_DIM):
        half = slice(hp * MXU_DIM, (hp + 1) * MXU_DIM)
        ys = []
        for j in range(hp * MXU_DIM // LANES, (hp + 1) * MXU_DIM // LANES):
            ls = slice(j * LANES, (j + 1) * LANES)
            acc = jnp.zeros((tm, LANES), F32) + cb_ref[:, ls]
            for t in range(CONV_KERNEL):
                acc = acc + cw_ref[t:t + 1, ls] * ubuf[j, base + t:base + t + tm, :]
            ys.append(acc)
            yield
        y = jnp.concatenate(ys, axis=-1)
        mu = _dot(y, gmat_ref[...])
        d = y - mu
        var = _dot(d * d, gmat_ref[...])
        un_half = d * lax.rsqrt(var + EPS) * (0.5 * gnw_ref[:, half]) + 0.5 * gnb_ref[:, half]
        hoa[:, half] = _silu_of_twice(un_half).astype(hoa.dtype)
        yield
    ubuf[:, 0:CONV_HALO, :] = ubuf[:, tm:tm + CONV_HALO, :]

    sbase = SC_HALO - (GDN_SHORT_CONV - 1)
    outs = (hq, hk, hv)
    for part in range(3):
        for hp in range(GDN_WIDTH // MXU_DIM):
            heads_out = []
            for j in range((part * GDN_WIDTH + hp * MXU_DIM) // LANES, (part * GDN_WIDTH + (hp + 1) * MXU_DIM) // LANES):
                ls = slice(j * LANES, (j + 1) * LANES)
                acc = (0.5 * scw_ref[0:1, ls]) * qbuf[j, sbase:sbase + tm, :]
                for t in range(1, GDN_SHORT_CONV):
                    acc = acc + (0.5 * scw_ref[t:t + 1, ls]) * qbuf[j, sbase + t:sbase + t + tm, :]
                a = _silu_of_twice(acc)
                if part < 2:
                    a = a * lax.rsqrt(jnp.sum(a * a, axis=-1, keepdims=True) + EPS)
                heads_out.append(a)
            outs[part][:, hp * MXU_DIM:(hp + 1) * MXU_DIM] = jnp.concatenate(heads_out, axis=-1)
            yield
    qbuf[:, 0:SC_HALO, :] = qbuf[:, tm:tm + SC_HALO, :]

    hz[...] = z
    hgc[...] = packed
    hgr[...] = packed_t
    yield


def _gdn_heads(hq, hk, hv, hz, hgc, hgr, nw_ref, state):
    rows = hq.shape[0]
    r = GDN_GROUP
    assert r == GDN_HEAD_DIM and rows % r == 0 and r % GDN_CHUNK == 0
    nchunk = rows // GDN_CHUNK
    row = lax.broadcasted_iota(jnp.int32, (r, r), 0)
    col = lax.broadcasted_iota(jnp.int32, (r, r), 1)
    log_c = GDN_CHUNK.bit_length() - 1
    same_chunk = (row >> log_c) == (col >> log_c)
    causal = same_chunk & (row >= col)
    strict = same_chunk & (row > col)
    eye = (row == col).astype(F32)

    gc_all = hgc[...]
    gr_all = hgr[...]
    scale = GDN_HEAD_DIM ** -0.5

    heads = range(GDN_HEADS)
    lanes = [slice(hd * GDN_HEAD_DIM, (hd + 1) * GDN_HEAD_DIM) for hd in heads]
    units = [(slice(g * r, (g + 1) * r), hd) for g in range(rows // r) for hd in heads]
    un = range(len(units))
    q = [hq[rs, lanes[hd]] * scale for rs, hd in units]
    k = [hk[rs, lanes[hd]] for rs, hd in units]
    v = [hv[rs, lanes[hd]] for rs, hd in units]
    gcb = [jnp.broadcast_to(gc_all[rs, GDN_HEADS + hd:GDN_HEADS + hd + 1], (r, GDN_HEAD_DIM))
           for rs, hd in units]
    bb = [jnp.broadcast_to(gc_all[rs, hd:hd + 1], (r, GDN_HEAD_DIM)) for rs, hd in units]
    decay = []
    for u, (rs, hd) in enumerate(units):
        gcj = jnp.broadcast_to(gr_all[GDN_HEADS + hd:GDN_HEADS + hd + 1, rs], (r, r))
        decay.append(jnp.where(causal, jnp.exp(gcb[u] - gcj), 0.0))

    kb = [k[u] * bb[u] for u in un]
    kk = [lax.dot_general(jnp.concatenate([kb[u], q[u]], axis=0).astype(BF16), k[u].astype(BF16),
                          (((1,), (1,)), ((), ())), preferred_element_type=F32) for u in un]
    a_mat = [jnp.where(strict, kk[u][:r] * decay[u], 0.0) for u in un]
    qk = [kk[u][r:] * decay[u] for u in un]

    pair = ((row ^ col) >> 1) == 0
    xinv = [eye - jnp.where(pair, a_mat[u], 0.0) for u in un]
    yield
    s = 2
    while s < GDN_CHUNK:
        ls2 = s.bit_length()
        m = (((row ^ col) >> ls2) == 0) & ((row & s) != 0) & ((col & s) == 0)
        xb = [xinv[u].astype(BF16) for u in un]
        ax = [_dot(jnp.where(m, a_mat[u], 0.0), xb[u]) for u in un]
        xinv = [xinv[u] - _dot(xb[u], ax[u]) for u in un]
        s *= 2
        yield

    egc = [jnp.exp(gcb[u]) for u in un]
    uw = [_dot(xinv[u], jnp.concatenate([v[u] * bb[u], kb[u] * egc[u]], axis=-1)) for u in un]
    qdec = [q[u] * egc[u] for u in un]
    yield

    terms = []
    for c in range(nchunk):
        g, cs = divmod(c * GDN_CHUNK, r)
        rs = slice(cs, cs + GDN_CHUNK)
        last = cs + GDN_CHUNK - 1
        per_head = []
        for hd in heads:
            u = g * GDN_HEADS + hd
            gl = gcb[u][last:last + 1, :]
            kdec = k[u][rs] * jnp.exp(gl - gcb[u][rs])
            kuw = lax.dot_general(kdec.astype(BF16), uw[u][rs].astype(BF16), (((0,), (0,)), ((), ())),
                                  preferred_element_type=F32)
            per_head.append((u, rs, jnp.exp(gl), kuw[:, :GDN_HEAD_DIM], kuw[:, GDN_HEAD_DIM:]))
        terms.append(per_head)
    yield

    st = [state[hd] for hd in heads]
    vnews = [[] for _ in un]
    qss = [[] for _ in un]
    for per_head in terms:
        prods = [_dot(jnp.concatenate([kw, uw[u][rs, GDN_HEAD_DIM:], qdec[u][rs]], axis=0), st[hd])
                 for hd, (u, rs, _, _, kw) in zip(heads, per_head)]
        for hd, (u, rs, e, ku, _) in zip(heads, per_head):
            vnews[u].append(uw[u][rs, :GDN_HEAD_DIM] - prods[hd][GDN_HEAD_DIM:GDN_HEAD_DIM + GDN_CHUNK])
            qss[u].append(prods[hd][GDN_HEAD_DIM + GDN_CHUNK:])
            st[hd] = st[hd] * e + ku - prods[hd][:GDN_HEAD_DIM]
        yield
    for hd in heads:
        state[hd] = st[hd]
    outs = []
    for hd in heads:
        o = jnp.concatenate([jnp.concatenate(qss[u], axis=0) + _dot(qk[u], jnp.concatenate(vnews[u], axis=0))
                             for u in range(hd, len(units), GDN_HEADS)], axis=0)
        o = o * lax.rsqrt(jnp.mean(o * o, axis=-1, keepdims=True) + EPS) * nw_ref[...]
        outs.append((o * _silu_of_twice(hz[:, lanes[hd]])).astype(BF16))
    return outs


def _stage_out(x_ref, mod_ref, gnw_ref, wo_ref, nfw_ref, wfi_ref, wfo_ref, nlw_ref,
               hoa, hq, hk, hv, hz, hgc, hgr, state, o_ref, *, final_norm):
    mix = jnp.dot(hoa[...], wo_ref[0:CONV_WIDTH, :], preferred_element_type=F32)
    ob = yield from _gdn_heads(hq, hk, hv, hz, hgc, hgr, gnw_ref, state)
    mix = jnp.dot(jnp.concatenate(ob, axis=-1), wo_ref[CONV_WIDTH:, :], preferred_element_type=F32) + mix
    x = x_ref[0] + mod_ref[0, 2:3, :] * mix
    hb = _rms_mod(x, nfw_ref[...], mod_ref[0, 3:4, :], mod_ref[0, 4:5, :]).astype(BF16)
    yield
    ffn = None
    for j in range(D_FF // FF_CHUNK):
        cs = slice(j * FF_CHUNK, (j + 1) * FF_CHUNK)
        gate_half = jnp.dot(hb, wfi_ref[:, cs], preferred_element_type=F32).astype(BF16)
        up = jnp.dot(hb, wfi_ref[:, D_FF + j * FF_CHUNK:D_FF + (j + 1) * FF_CHUNK],
                     preferred_element_type=F32).astype(BF16)
        part = jnp.dot(_silu_of_twice(gate_half) * up, wfo_ref[cs, :], preferred_element_type=F32)
        ffn = part if ffn is None else part + ffn
        yield
    x = x + mod_ref[0, 5:6, :] * ffn
    o_ref[0] = _rms(x, nlw_ref[...]) if final_norm else x
    yield


N_IN_ARGS = 12
N_OUT_ARGS = 6

PIECE_ORDER = "oiooioioooooiooioo" + "iiii" + "o" * (D_FF // FF_CHUNK) + "iiii" + "o" + "iii"


def _layer_kernel(*refs, tiles_per_seq, final_norm):
    x_in, mod_in, x_out, mod_out = refs[0:4]
    in_args = refs[4:4 + N_IN_ARGS]
    out_args = refs[4 + N_IN_ARGS:4 + N_IN_ARGS + N_OUT_ARGS]
    o_ref = refs[4 + N_IN_ARGS + N_OUT_ARGS]
    hoa, hq, hk, hv, hz, hgc, hgr, ubuf, qbuf, state = refs[5 + N_IN_ARGS + N_OUT_ARGS:]
    hand = (hoa, hq, hk, hv, hz, hgc, hgr)
    t = pl.program_id(0)

    @pl.when(t == 0)
    def _():
        for ref in hand:
            ref[...] = jnp.zeros(ref.shape, ref.dtype)

    @pl.when(t % tiles_per_seq == 0)
    def _():
        ubuf[:, 0:CONV_HALO, :] = jnp.zeros((ubuf.shape[0], CONV_HALO, LANES), F32)
        qbuf[:, 0:SC_HALO, :] = jnp.zeros((qbuf.shape[0], SC_HALO, LANES), F32)

    @pl.when(t % tiles_per_seq == 1 % tiles_per_seq)
    def _():
        state[...] = jnp.zeros(state.shape, F32)

    gens = {"o": _stage_out(x_out, mod_out, *out_args, *hand, state, o_ref, final_norm=final_norm),
            "i": _stage_in(x_in, mod_in, *in_args, *hand, ubuf, qbuf)}
    for which in PIECE_ORDER:
        next(gens[which])
    done = object()
    assert all(next(g, done) is done for g in gens.values())


def _layer(x, mod, in_ws, out_ws, final_norm):
    bsz, seq, d = x.shape
    tm = TM
    n = seq // tm
    last = bsz * n - 1
    steps = bsz * n + 1

    def tile_in(t):
        tt = jnp.minimum(t, last)
        return tt // n, tt % n

    def tile_out(t):
        tt = jnp.maximum(t - 1, 0)
        return tt // n, tt % n

    once = lambda a: pl.BlockSpec(a.shape, lambda t: (0,) * a.ndim, pipeline_mode=pl.Buffered(1))
    x_spec = lambda tile: pl.BlockSpec((1, tm, d), lambda t: (*tile(t), 0))
    mod_spec = lambda tile: pl.BlockSpec((1, N_MOD, d), lambda t: (tile(t)[0], 0, 0))
    tri = _chunk_tri_const(tm, GDN_CHUNK).astype(BF16)
    gmat = _block_diag_const(MXU_DIM, CONV_WIDTH // CONV_GROUPS, 1.0 / (CONV_WIDTH // CONV_GROUPS)).astype(BF16)
    (nw, w_in, w_ba, cw, cb, gnw, gnb, scw, alog, dtb) = in_ws
    in_full = (nw, w_in, w_ba, cw, cb, gnw, gnb, gmat, scw, alog, dtb, tri)
    assert len(in_full) == N_IN_ARGS and len(out_ws) == N_OUT_ARGS
    return pl.pallas_call(
        functools.partial(_layer_kernel, tiles_per_seq=n, final_norm=final_norm),
        out_shape=jax.ShapeDtypeStruct((bsz, seq, d), F32),
        grid=(steps,),
        in_specs=[x_spec(tile_in), mod_spec(tile_in), x_spec(tile_out), mod_spec(tile_out)]
                 + [once(a) for a in in_full] + [once(a) for a in out_ws],
        out_specs=x_spec(tile_out),
        scratch_shapes=[pltpu.VMEM((tm, CONV_WIDTH), BF16),
                        pltpu.VMEM((tm, GDN_WIDTH), F32), pltpu.VMEM((tm, GDN_WIDTH), F32),
                        pltpu.VMEM((tm, GDN_WIDTH), F32), pltpu.VMEM((tm, GDN_WIDTH), F32),
                        pltpu.VMEM((tm, LANES), F32), pltpu.VMEM((SUBLANES, tm), F32),
                        pltpu.VMEM((CONV_WIDTH // LANES, CONV_HALO + tm, LANES), F32),
                        pltpu.VMEM((3 * GDN_WIDTH // LANES, SC_HALO + tm, LANES), F32),
                        pltpu.VMEM((GDN_HEADS, GDN_HEAD_DIM, GDN_HEAD_DIM), F32)],
        compiler_params=pltpu.CompilerParams(dimension_semantics=("arbitrary",),
                                             vmem_limit_bytes=VMEM_LIMIT),
        name="layer",
    )(x, mod, x, mod, *in_full, *out_ws)


def _block_diag_const(n, blk, val):
    idx = np.arange(n) // blk
    return jnp.asarray((idx[:, None] == idx[None, :]).astype(np.float32) * val)


def _chunk_tri_const(n, blk):
    i = np.arange(n)
    m = ((i[:, None] // blk) == (i[None, :] // blk)) & (i[:, None] >= i[None, :])
    return jnp.asarray(m.astype(np.float32))


def _pad_lanes(a, n):
    return jnp.pad(a, [(0, 0)] * (a.ndim - 1) + [(0, n - a.shape[-1])])


def kernel(x, c, w_ada, b_ada, norm_mix_w, w_in, conv_w, conv_b, conv_gn_w, conv_gn_b, gdn_conv_w, gdn_a_log, gdn_dt_bias, gdn_norm_w, w_out, norm_ffn_w, w_ffn_in, w_ffn_out, norm_final_w):
    depth = w_ada.shape[0]
    bsz = x.shape[0]
    for layer in range(depth):
        mod = _adaln(c, w_ada[layer], b_ada[layer]).reshape(bsz, N_MOD, D_MODEL)
        o1, o2, o3 = W_IN_SPLITS
        cols = jnp.arange(w_in.shape[-1])
        z_half = jnp.where((cols >= o2) & (cols < o3), 0.5, 1.0).astype(F32)
        w_ba = _pad_lanes(w_in[layer][:, o3:], LANES).astype(BF16)
        head_row = lambda a: _pad_lanes(jnp.concatenate([jnp.zeros_like(a), a])[None, :], LANES)
        in_ws = (norm_mix_w[layer][None, :], (w_in[layer] * z_half).astype(BF16), w_ba,
                 conv_w[layer], conv_b[layer][None, :],
                 conv_gn_w[layer][None, :], conv_gn_b[layer][None, :], gdn_conv_w[layer],
                 head_row(gdn_a_log[layer]), head_row(gdn_dt_bias[layer]))
        gate_half = jnp.where(jnp.arange(2 * D_FF) < D_FF, 0.5, 1.0).astype(F32)
        out_ws = (gdn_norm_w[layer][None, :], w_out[layer].astype(BF16), norm_ffn_w[layer][None, :],
                  (w_ffn_in[layer] * gate_half).astype(BF16), w_ffn_out[layer].astype(BF16),
                  norm_final_w[None, :])
        x = _layer(x, mod, in_ws, out_ws, final_norm=layer == depth - 1)
    return x
```
